```python
import jax, jax.numpy as jnp
from jax import lax
import numpy as np

D_MODEL = 1024
BATCH = 32
SEQ = 2048
DEPTH = 2

GRID_W = 64
CTX_LEN = 256
W_CONV = D_MODEL
W_LRU = D_MODEL
W_EVEN = W_CONV + W_LRU
N_LRU_HEADS = 8
LRU_HEAD_DIM = W_LRU // N_LRU_HEADS
CONV_WIDTH = 31
CONV_PAD = (15, 15)
SHORT_CONV_WIDTH = 4
SHORT_CONV_PAD = (2, 1)
LRU_C = 8.0
W_POOL = 2 * D_MODEL
POOL_WINDOWS = (2, 4, 8, 16)
N_POOL_GROUPS = len(POOL_WINDOWS)
POOL_GROUP_DIM = W_POOL // N_POOL_GROUPS
N_EVEN = (DEPTH + 1) // 2
N_ODD = DEPTH // 2
RMS_EPS = 1e-6
LN_EPS = 1e-5

kernel_name = "hybrid_conv_rglru_pool_prefix_dit"


def rmsnorm(x, g):
    xf = x.astype(jnp.float32)
    y = xf * lax.rsqrt(jnp.mean(xf * xf, axis=-1, keepdims=True) + RMS_EPS)
    return (y * g.astype(jnp.float32)).astype(x.dtype)


def layernorm(x, g, b):
    xf = x.astype(jnp.float32)
    mu = jnp.mean(xf, axis=-1, keepdims=True)
    var = jnp.mean(jnp.square(xf - mu), axis=-1, keepdims=True)
    y = (xf - mu) * lax.rsqrt(var + LN_EPS) * g.astype(jnp.float32) + b.astype(jnp.float32)
    return y.astype(x.dtype)


def modulate(h, shift, scale):
    return h * (1 + scale) + shift


def depthwise_conv(u, w, b, pad):
    y = lax.conv_general_dilated(
        u, w[:, None, :].astype(u.dtype), window_strides=(1,), padding=(pad,),
        dimension_numbers=("NWC", "WIO", "NWC"), feature_group_count=u.shape[-1])
    return y + b.astype(u.dtype)


def conformer_conv(va, vg, p):
    u = va * jax.nn.sigmoid(vg)
    u = depthwise_conv(u, p["conv_w"], p["conv_b"], CONV_PAD)
    return jax.nn.silu(layernorm(u, p["ln_g"], p["ln_b"]))


def lru_coeffs(u2, p):
    bn, n = u2.shape[0], u2.shape[1]
    uf = u2.astype(jnp.float32)
    uh = uf.reshape(bn, n, 2, N_LRU_HEADS, LRU_HEAD_DIM)
    gr = jnp.einsum("bnkhi,khij->bnkhj", uh, p["w_r"].astype(jnp.float32)).reshape(bn, n, 2, W_LRU)
    gi = jnp.einsum("bnkhi,khij->bnkhj", uh, p["w_i"].astype(jnp.float32)).reshape(bn, n, 2, W_LRU)
    r = jax.nn.sigmoid(gr + p["b_r"].astype(jnp.float32))
    i = jax.nn.sigmoid(gi + p["b_i"].astype(jnp.float32))
    log_a = -LRU_C * r * jax.nn.softplus(-p["lam"].astype(jnp.float32))
    a = jnp.exp(log_a)
    bx = jnp.sqrt(-jnp.expm1(2.0 * log_a)) * (i * uf)
    return a, bx


def lru_scan(a, bx, h0, with_outputs):
    def step(h, ab):
        at, bt = ab
        h = at * h + bt
        return h, (h if with_outputs else None)
    hT, hs = lax.scan(step, h0, (jnp.swapaxes(a, 0, 1), jnp.swapaxes(bx, 0, 1)))
    return hT, hs


def rglru_branch(xb, p, h0, with_outputs):
    u = depthwise_conv(xb, p["sconv_w"], p["sconv_b"], SHORT_CONV_PAD)
    u2 = jnp.stack([u, u[:, ::-1]], axis=2)
    a, bx = lru_coeffs(u2, p)
    hT, hs = lru_scan(a, bx, h0, with_outputs)
    if not with_outputs:
        return None, hT
    y = hs[:, :, 0] + hs[::-1, :, 1]
    return jnp.swapaxes(y, 0, 1).astype(xb.dtype), hT


def even_mixer(h, h0, p):
    z = h @ p["w_in"]
    va, vg, ga, xb, gb = jnp.split(
        z, [W_CONV, 2 * W_CONV, 3 * W_CONV, 3 * W_CONV + W_LRU], axis=-1)
    y_a = conformer_conv(va, vg, p) * jax.nn.silu(ga)
    y_b, hT = rglru_branch(xb, p, h0, True)
    y_b = y_b * jax.nn.silu(gb)
    return jnp.concatenate([y_a, y_b], axis=-1) @ p["w_out"], hT


def multiscale_pool(u):
    n = u.shape[1]
    uf = u.astype(jnp.float32)
    s = jnp.pad(jnp.cumsum(uf, axis=1), ((0, 0), (1, 0), (0, 0)))
    t = jnp.arange(n)
    parts = []
    for g, w in enumerate(POOL_WINDOWS):
        lo = jnp.clip(t - w // 2, 0, n)
        hi = jnp.clip(t + w // 2, 0, n)
        sg = s[:, :, g * POOL_GROUP_DIM:(g + 1) * POOL_GROUP_DIM]
        cnt = (hi - lo).astype(jnp.float32)[None, :, None]
        parts.append((sg[:, hi] - sg[:, lo]) / cnt)
    return (jnp.concatenate(parts, axis=-1) - uf).astype(u.dtype)


def odd_mixer(h, p, on_grid):
    bn, n = h.shape[0], h.shape[1]
    z = h @ p["w_in"]
    u, g = jnp.split(z, [W_POOL], axis=-1)
    if on_grid:
        rows = n // GRID_W
        d = multiscale_pool(u.reshape(bn * rows, GRID_W, W_POOL)).reshape(bn, n, W_POOL)
    else:
        d = multiscale_pool(u)
    d = d.reshape(bn, n, N_POOL_GROUPS, POOL_GROUP_DIM)
    y = jnp.einsum("bngi,gij->bngj", d, p["w_grp"]).reshape(bn, n, W_POOL) * p["scale"]
    return (y * jax.nn.silu(g)) @ p["w_out"]


def setup_inputs(seed: int = 0) -> dict:
    key = jax.random.key(seed)
    ks = jax.random.split(key, 32)
    f32 = jnp.float32

    def nrm(k, shape, s):
        return s * jax.random.normal(k, shape, f32)

    ua = jax.random.uniform(ks[20], (N_EVEN, 2, W_LRU), f32, 0.9, 0.999)
    pa = ua ** (1.0 / LRU_C)
    lam = jnp.log(pa) - jnp.log1p(-pa)
    return {
        "x": nrm(ks[0], (BATCH, SEQ, D_MODEL), 1.0),
        "c": nrm(ks[1], (BATCH, D_MODEL), 1.0),
        "ctx": nrm(ks[2], (BATCH, CTX_LEN, D_MODEL), 1.0),
        "c_ctx": nrm(ks[3], (D_MODEL,), 1.0),
        "norm_g": 1.0 + nrm(ks[4], (DEPTH, D_MODEL), 0.1),
        "mod_w": nrm(ks[5], (DEPTH, D_MODEL, 3 * D_MODEL), 0.5 * D_MODEL ** -0.5),
        "mod_b": nrm(ks[6], (DEPTH, 3 * D_MODEL), 0.02),
        "ev_w_in": nrm(ks[7], (N_EVEN, D_MODEL, 3 * W_CONV + 2 * W_LRU), D_MODEL ** -0.5),
        "ev_conv_w": nrm(ks[8], (N_EVEN, CONV_WIDTH, W_CONV), CONV_WIDTH ** -0.5),
        "ev_conv_b": nrm(ks[9], (N_EVEN, W_CONV), 0.02),
        "ev_ln_g": 1.0 + nrm(ks[10], (N_EVEN, W_CONV), 0.1),
        "ev_ln_b": nrm(ks[11], (N_EVEN, W_CONV), 0.02),
        "ev_sconv_w": nrm(ks[12], (N_EVEN, SHORT_CONV_WIDTH, W_LRU), SHORT_CONV_WIDTH ** -0.5),
        "ev_sconv_b": nrm(ks[13], (N_EVEN, W_LRU), 0.02),
        "ev_w_r": nrm(ks[14], (N_EVEN, 2, N_LRU_HEADS, LRU_HEAD_DIM, LRU_HEAD_DIM), LRU_HEAD_DIM ** -0.5),
        "ev_b_r": nrm(ks[15], (N_EVEN, 2, W_LRU), 0.02),
        "ev_w_i": nrm(ks[16], (N_EVEN, 2, N_LRU_HEADS, LRU_HEAD_DIM, LRU_HEAD_DIM), LRU_HEAD_DIM ** -0.5),
        "ev_b_i": nrm(ks[17], (N_EVEN, 2, W_LRU), 0.02),
        "ev_lam": lam,
        "ev_w_out": nrm(ks[18], (N_EVEN, W_EVEN, D_MODEL), W_EVEN ** -0.5),
        "od_w_in": nrm(ks[21], (N_ODD, D_MODEL, 2 * W_POOL), D_MODEL ** -0.5),
        "od_w_grp": nrm(ks[22], (N_ODD, N_POOL_GROUPS, POOL_GROUP_DIM, POOL_GROUP_DIM), POOL_GROUP_DIM ** -0.5),
        "od_scale": 1.0 + nrm(ks[23], (N_ODD, W_POOL), 0.1),
        "od_w_out": nrm(ks[24], (N_ODD, W_POOL, D_MODEL), W_POOL ** -0.5),
        "final_g": 1.0 + nrm(ks[25], (D_MODEL,), 0.1),
    }


def reference(x, c, ctx, c_ctx, norm_g, mod_w, mod_b,
              ev_w_in, ev_conv_w, ev_conv_b, ev_ln_g, ev_ln_b, ev_sconv_w, ev_sconv_b,
              ev_w_r, ev_b_r, ev_w_i, ev_b_i, ev_lam, ev_w_out,
              od_w_in, od_w_grp, od_scale, od_w_out, final_g):
    bn = x.shape[0]
    sc = jax.nn.silu(c)
    scc = jax.nn.silu(c_ctx)
    for l in range(DEPTH):
        ctx_out_needed = any(j % 2 == 0 for j in range(l + 1, DEPTH))
        shift, scale, gate = jnp.split(sc @ mod_w[l] + mod_b[l], 3, axis=-1)
        hx = modulate(rmsnorm(x, norm_g[l]), shift[:, None], scale[:, None])
        if ctx_out_needed or l % 2 == 0:
            shift_c, scale_c, gate_c = jnp.split(scc @ mod_w[l] + mod_b[l], 3, axis=-1)
            hc = modulate(rmsnorm(ctx, norm_g[l]), shift_c, scale_c)
        j = l // 2
        if l % 2 == 0:
            pe = {"w_in": ev_w_in[j], "conv_w": ev_conv_w[j], "conv_b": ev_conv_b[j],
                  "ln_g": ev_ln_g[j], "ln_b": ev_ln_b[j], "sconv_w": ev_sconv_w[j],
                  "sconv_b": ev_sconv_b[j], "w_r": ev_w_r[j], "b_r": ev_b_r[j],
                  "w_i": ev_w_i[j], "b_i": ev_b_i[j], "lam": ev_lam[j], "w_out": ev_w_out[j]}
            h0 = jnp.zeros((bn, 2, W_LRU), jnp.float32)
            if ctx_out_needed:
                out_c, h_ctx = even_mixer(hc, h0, pe)
            else:
                xb_c = hc @ pe["w_in"][:, 3 * W_CONV:3 * W_CONV + W_LRU]
                _, h_ctx = rglru_branch(xb_c, pe, h0, False)
            out_x, _ = even_mixer(hx, h_ctx, pe)
        else:
            po = {"w_in": od_w_in[j], "w_grp": od_w_grp[j], "scale": od_scale[j], "w_out": od_w_out[j]}
            if ctx_out_needed:
                out_c = odd_mixer(hc, po, False)
            out_x = odd_mixer(hx, po, True)
        x = x + gate[:, None] * out_x
        if ctx_out_needed:
            ctx = ctx + gate_c * out_c
    return rmsnorm(x, final_g)
```

```python
import functools

import jax
import jax.numpy as jnp
from jax import lax
from jax.experimental import pallas as pl
from jax.experimental.pallas import tpu as pltpu

F32 = jnp.float32
BF16 = jnp.bfloat16

RMS_EPS = 1e-6
LN_EPS = 1e-5
LRU_C = 8.0
N_LRU_HEADS = 8
CONV_WIDTH = 31
CONV_HALF = 15
SHORT_CONV_WIDTH = 4
GRID_W = 64
POOL_WINDOWS = (2, 4, 8, 16)

LANES = 128
SUBLANES = 8
VMEM_LIMIT_BYTES = 56 * 1024 * 1024


def _params(n_axes):
    return pltpu.CompilerParams(
        dimension_semantics=("arbitrary",) * n_axes,
        vmem_limit_bytes=VMEM_LIMIT_BYTES)


def _sigmoid(x):
    return 0.5 * jnp.tanh(0.5 * x) + 0.5


def _silu(x):
    return x * _sigmoid(x)


def _mod_kernel(c_ref, w_ref, b_ref, o_ref):
    s = _silu(c_ref[...])
    o_ref[0] = jnp.dot(s, w_ref[0], preferred_element_type=F32,
                       precision=lax.Precision.HIGHEST) + b_ref[0]


def _modulation(cc, mod_w, mod_b):
    depth, d, d3 = mod_w.shape
    r = cc.shape[0]
    tn = 1024
    return pl.pallas_call(
        _mod_kernel,
        grid=(depth, d3 // tn),
        in_specs=[pl.BlockSpec((r, d), lambda l, j: (0, 0)),
                  pl.BlockSpec((1, d, tn), lambda l, j: (l, 0, j)),
                  pl.BlockSpec((1, 1, tn), lambda l, j: (l, 0, j))],
        out_specs=pl.BlockSpec((1, r, tn), lambda l, j: (l, 0, j)),
        out_shape=jax.ShapeDtypeStruct((depth, r, d3), F32),
        compiler_params=_params(2),
        name="modulation",
    )(cc, mod_w, mod_b.reshape(depth, 1, d3))


def _inproj_kernel(x_ref, g_ref, shift_ref, scale_ref, w_ref, *o_refs, plan, nb):
    x = x_ref[...]
    tm, d = x.shape
    ms = jnp.mean(x * x, axis=-1, keepdims=True)
    y = x * lax.rsqrt(ms + RMS_EPS) * g_ref[...]
    y = y.reshape(tm // nb, nb, d) * (1.0 + scale_ref[...])[None] + shift_ref[...][None]
    hx = y.reshape(tm, d).astype(BF16)

    def proj(c0, width):
        return jnp.dot(hx, w_ref[:, c0:c0 + width], preferred_element_type=F32)

    for (kind, cols, width), o_ref in zip(plan, o_refs):
        if kind == "glu":
            o = proj(cols[0], width) * _sigmoid(proj(cols[1], width))
        elif kind == "silu":
            o = _silu(proj(cols[0], width))
        else:
            o = proj(cols[0], width)
        o_ref[...] = o.astype(o_ref.dtype)


def _inproj(rows, norm_g, shift, scale, w_bf16, plan, tm, name):
    n, d = rows.shape
    nb = shift.shape[0]
    kern = functools.partial(_inproj_kernel, plan=plan, nb=nb)
    const = lambda i: (0, 0)
    return pl.pallas_call(
        kern,
        grid=(n // tm,),
        in_specs=[pl.BlockSpec((tm, d), lambda i: (i, 0)),
                  pl.BlockSpec((1, d), const),
                  pl.BlockSpec((nb, d), const),
                  pl.BlockSpec((nb, d), const),
                  pl.BlockSpec(w_bf16.shape, const)],
        out_specs=[pl.BlockSpec((tm, width), lambda i: (i, 0)) for _, _, width in plan],
        out_shape=[jax.ShapeDtypeStruct((n, width), BF16) for _, _, width in plan],
        compiler_params=_params(1),
        name=name,
    )(rows, norm_g.reshape(1, d), shift, scale, w_bf16)


def _lru_kernel(prev_ref, cur_ref, next_ref, cw_ref, cb_ref, wg_ref, br_ref, bi_ref,
                lam_ref, h0_ref, hs_ref, ht_ref, xe_s, a_s, b_s, h_s, *, nb, reverse):
    i = pl.program_id(0)
    nc = pl.num_programs(0)
    c = nc - 1 - i if reverse else i
    r, w = cur_ref.shape
    steps = r // nb
    lead = (SHORT_CONV_WIDTH - 2) * nb

    @pl.when(i == 0)
    def _():
        h_s[...] = h0_ref[...]

    @pl.when(c > 0)
    def _():
        xe_s[0:lead] = prev_ref[...].astype(F32)

    @pl.when(c == 0)
    def _():
        xe_s[0:lead] = jnp.zeros((lead, w), F32)

    xe_s[lead:lead + r] = cur_ref[...].astype(F32)

    @pl.when(c < nc - 1)
    def _():
        xe_s[lead + r:lead + r + nb] = next_ref[...].astype(F32)

    @pl.when(c == nc - 1)
    def _():
        xe_s[lead + r:lead + r + nb] = jnp.zeros((nb, w), F32)

    lam = lam_ref[...]
    neg = -lam
    softplus_neg = jnp.maximum(neg, 0.0) + jnp.log1p(jnp.exp(-jnp.abs(neg)))
    decay = -LRU_C * softplus_neg

    hd = w // N_LRU_HEADS
    for h in range(N_LRU_HEADS):
        sl = slice(h * hd, (h + 1) * hd)
        u = cb_ref[:, sl] + cw_ref[0:1, sl] * xe_s[0:r, sl]
        for k in range(1, SHORT_CONV_WIDTH):
            u = u + cw_ref[k:k + 1, sl] * xe_s[k * nb:k * nb + r, sl]
        g = jnp.dot(u.astype(BF16), wg_ref[h], preferred_element_type=F32)
        rg = _sigmoid(g[:, :hd] + br_ref[:, sl])
        ig = _sigmoid(g[:, hd:] + bi_ref[:, sl])
        log_a = rg * decay[:, sl]
        a = jnp.exp(log_a)
        a_s[:, sl] = a
        b_s[:, sl] = jnp.sqrt(-jnp.tanh(log_a) * (a * a + 1.0)) * (ig * u)

    def step(s, hcur):
        t = steps - 1 - s if reverse else s
        rows = pl.ds(pl.multiple_of(t * nb, nb), nb)
        hnew = a_s[rows, :] * hcur + b_s[rows, :]
        hs_ref[rows, :] = hnew
        return hnew

    hfin = lax.fori_loop(0, steps, step, h_s[...])
    h_s[...] = hfin
    ht_ref[...] = hfin


def _lru_scan(xb, sconv_w, sconv_b, wg, b_r, b_i, lam, h0, *, steps, reverse, name):
    n, w = xb.shape
    nb = h0.shape[0]
    r = steps * nb
    nc = n // r
    lead = (SHORT_CONV_WIDTH - 2) * nb
    chunk = (lambda i: nc - 1 - i) if reverse else (lambda i: i)
    const = lambda i: (0, 0)
    kern = functools.partial(_lru_kernel, nb=nb, reverse=reverse)
    return pl.pallas_call(
        kern,
        grid=(nc,),
        in_specs=[
            pl.BlockSpec((lead, w), lambda i: (jnp.maximum(chunk(i) * (r // lead) - 1, 0), 0)),
            pl.BlockSpec((r, w), lambda i: (chunk(i), 0)),
            pl.BlockSpec((nb, w), lambda i: (jnp.minimum((chunk(i) + 1) * (r // nb), n // nb - 1), 0)),
            pl.BlockSpec((SHORT_CONV_WIDTH, w), const),
            pl.BlockSpec((1, w), const),
            pl.BlockSpec(wg.shape, lambda i: (0, 0, 0)),
            pl.BlockSpec((1, w), const),
            pl.BlockSpec((1, w), const),
            pl.BlockSpec((1, w), const),
            pl.BlockSpec((nb, w), const),
        ],
        out_specs=[pl.BlockSpec((r, w), lambda i: (chunk(i), 0)),
                   pl.BlockSpec((nb, w), const)],
        out_shape=[jax.ShapeDtypeStruct((n, w), F32),
                   jax.ShapeDtypeStruct((nb, w), F32)],
        scratch_shapes=[pltpu.VMEM((lead + r + nb, w), F32),
                        pltpu.VMEM((r, w), F32),
                        pltpu.VMEM((r, w), F32),
                        pltpu.VMEM((nb, w), F32)],
        compiler_params=_params(1),
        name=name,
    )(xb, xb, xb, sconv_w, sconv_b.reshape(1, w), wg, b_r.reshape(1, w), b_i.reshape(1, w),
      lam.reshape(1, w), h0)


def _even_out_kernel(uprev_ref, ucur_ref, unext_ref, sga_ref, sgb_ref, hf_ref, hb_ref, x_ref,
                     gate_ref, cw_ref, cb_ref, lg_ref, lb_ref, wo_ref, o_ref, ue_s, v_s,
                     *, nb, row_tile):
    c = pl.program_id(0)
    nc = pl.num_programs(0)
    r, w = ucur_ref.shape
    halo = CONV_HALF * nb

    @pl.when(c > 0)
    def _():
        ue_s[0:r] = uprev_ref[...].astype(F32)

    @pl.when(c == 0)
    def _():
        ue_s[0:r] = jnp.zeros((r, w), F32)

    ue_s[r:2 * r] = ucur_ref[...].astype(F32)

    @pl.when(c < nc - 1)
    def _():
        ue_s[2 * r:3 * r] = unext_ref[...].astype(F32)

    @pl.when(c == nc - 1)
    def _():
        ue_s[2 * r:3 * r] = jnp.zeros((r, w), F32)

    n_row_tiles = r // row_tile
    sub = row_tile // SUBLANES
    for lt in range(w // LANES):
        ls = slice(lt * LANES, (lt + 1) * LANES)
        wk = [jnp.broadcast_to(cw_ref[k:k + 1, ls], (SUBLANES, LANES))[None]
              for k in range(CONV_WIDTH)]
        bias = jnp.broadcast_to(cb_ref[:, ls], (SUBLANES, LANES))[None]

        def body(j, carry, ls=ls, wk=wk, bias=bias):
            base = pl.multiple_of(j * row_tile, row_tile) + (r - halo)
            acc = jnp.broadcast_to(bias, (sub, SUBLANES, LANES))
            for k in range(CONV_WIDTH):
                tap = ue_s[pl.ds(base + k * nb, row_tile), ls].reshape(sub, SUBLANES, LANES)
                acc = acc + wk[k] * tap
            v_s[pl.ds(pl.multiple_of(j * row_tile, row_tile), row_tile), ls] = (
                acc.reshape(row_tile, LANES))
            return carry

        lax.fori_loop(0, n_row_tiles, body, 0)

    v = v_s[...]
    mu = jnp.mean(v, axis=-1, keepdims=True)
    vc = v - mu
    var = jnp.mean(vc * vc, axis=-1, keepdims=True)
    yn = vc * lax.rsqrt(var + LN_EPS) * lg_ref[...] + lb_ref[...]
    ya = (_silu(yn) * sga_ref[...].astype(F32)).astype(BF16)
    yb = ((hf_ref[...] + hb_ref[...]) * sgb_ref[...].astype(F32)).astype(BF16)
    out = jnp.dot(ya, wo_ref[0:w, :], preferred_element_type=F32)
    out = out + jnp.dot(yb, wo_ref[w:2 * w, :], preferred_element_type=F32)
    d = out.shape[-1]
    x = x_ref[...].reshape(r // nb, nb, d)
    o_ref[...] = (x + gate_ref[...][None] * out.reshape(r // nb, nb, d)).reshape(r, d)


def _even_out(u, sga, sgb, hf, hb, rows, gate, conv_w, conv_b, ln_g, ln_b, w_out_bf16, *, steps):
    n, w = u.shape
    d = rows.shape[1]
    nb = gate.shape[0]
    r = steps * nb
    nc = n // r
    assert steps >= CONV_HALF
    const = lambda i: (0, 0)
    tile = lambda i: (i, 0)
    kern = functools.partial(_even_out_kernel, nb=nb, row_tile=64)
    return pl.pallas_call(
        kern,
        grid=(nc,),
        in_specs=[
            pl.BlockSpec((r, w), lambda i: (jnp.maximum(i - 1, 0), 0)),
            pl.BlockSpec((r, w), tile),
            pl.BlockSpec((r, w), lambda i: (jnp.minimum(i + 1, nc - 1), 0)),
            pl.BlockSpec((r, w), tile),
            pl.BlockSpec((r, w), tile),
            pl.BlockSpec((r, w), tile),
            pl.BlockSpec((r, w), tile),
            pl.BlockSpec((r, d), tile),
            pl.BlockSpec((nb, d), const),
            pl.BlockSpec((CONV_WIDTH, w), const),
            pl.BlockSpec((1, w), const),
            pl.BlockSpec((1, w), const),
            pl.BlockSpec((1, w), const),
            pl.BlockSpec(w_out_bf16.shape, const),
        ],
        out_specs=pl.BlockSpec((r, d), tile),
        out_shape=jax.ShapeDtypeStruct((n, d), F32),
        scratch_shapes=[pltpu.VMEM((3 * r, w), F32), pltpu.VMEM((r, w), F32)],
        compiler_params=_params(1),
        name="even_out",
    )(u, u, u, sga, sgb, hf, hb, rows, gate, conv_w, conv_b.reshape(1, w), ln_g.reshape(1, w),
      ln_b.reshape(1, w), w_out_bf16)


def _odd_out_kernel(u_ref, sg_ref, x_ref, gate_ref, wg_ref, scale_ref, wo_ref, fg_ref, o_ref,
                    ue_s, tot_s, acc_s, *, nbb):
    g = pl.program_id(2)
    gw, _, gd = u_ref.shape
    r = gw * nbb
    pad = (max(POOL_WINDOWS) // 2) * nbb
    d = x_ref.shape[-1]

    zeros = jnp.zeros((pad, gd), F32)
    ue_s[0:pad] = zeros
    ue_s[pad:pad + r] = u_ref[...].reshape(r, gd).astype(F32)
    ue_s[pad + r:pad + r + pad] = zeros

    t = lax.broadcasted_iota(jnp.int32, (r, LANES), 0) // nbb
    for gi, win in enumerate(POOL_WINDOWS):
        half = win // 2

        @pl.when(g == gi)
        def _(half=half):
            tot = ue_s[pad - half * nbb:pad - half * nbb + r]
            for off in range(-half + 1, half):
                tot = tot + ue_s[pad + off * nbb:pad + off * nbb + r]
            cnt = (jnp.minimum(t + half, gw) - jnp.maximum(t - half, 0)).astype(F32)
            inv = jnp.concatenate([1.0 / cnt] * (gd // LANES), axis=-1)
            tot_s[...] = tot * inv

    dlt = (tot_s[...] - ue_s[pad:pad + r]).astype(BF16)
    y = jnp.dot(dlt, wg_ref[0], preferred_element_type=F32)
    y = y * scale_ref[...] * sg_ref[...].reshape(r, gd).astype(F32)
    part = jnp.dot(y.astype(BF16), wo_ref[...], preferred_element_type=F32)

    @pl.when(g == 0)
    def _():
        acc_s[...] = part

    @pl.when(g > 0)
    def _():
        acc_s[...] += part

    @pl.when(g == pl.num_programs(2) - 1)
    def _():
        x = x_ref[...] + gate_ref[...][None] * acc_s[...].reshape(gw, nbb, d)
        ms = jnp.mean(x * x, axis=-1, keepdims=True)
        o_ref[...] = x * lax.rsqrt(ms + RMS_EPS) * fg_ref[...][None]


def _odd_out(u3, sg3, rows3, gate, w_grp_bf16, scale, w_out_bf16, final_g, *, nbb):
    seq, nb, wp = u3.shape
    d = rows3.shape[-1]
    n_groups = len(POOL_WINDOWS)
    gd = wp // n_groups
    pad = (max(POOL_WINDOWS) // 2) * nbb
    r = GRID_W * nbb
    grp = lambda i, j, g: (i, j, g)
    blk = lambda i, j, g: (i, j, 0)
    kern = functools.partial(_odd_out_kernel, nbb=nbb)
    return pl.pallas_call(
        kern,
        grid=(seq // GRID_W, nb // nbb, n_groups),
        in_specs=[
            pl.BlockSpec((GRID_W, nbb, gd), grp),
            pl.BlockSpec((GRID_W, nbb, gd), grp),
            pl.BlockSpec((GRID_W, nbb, d), blk),
            pl.BlockSpec((nbb, d), lambda i, j, g: (j, 0)),
            pl.BlockSpec((1, gd, gd), lambda i, j, g: (g, 0, 0)),
            pl.BlockSpec((1, gd), lambda i, j, g: (0, g)),
            pl.BlockSpec((gd, d), lambda i, j, g: (g, 0)),
            pl.BlockSpec((1, d), lambda i, j, g: (0, 0)),
        ],
        out_specs=pl.BlockSpec((GRID_W, nbb, d), blk),
        out_shape=jax.ShapeDtypeStruct((seq, nb, d), F32),
        scratch_shapes=[pltpu.VMEM((r + 2 * pad, gd), F32), pltpu.VMEM((r, gd), F32),
                        pltpu.VMEM((r, d), F32)],
        compiler_params=_params(3),
        name="odd_out",
    )(u3, sg3, rows3, gate, w_grp_bf16, scale.reshape(1, wp), w_out_bf16, final_g.reshape(1, d))


def kernel(x, c, ctx, c_ctx, norm_g, mod_w, mod_b, ev_w_in, ev_conv_w, ev_conv_b, ev_ln_g, ev_ln_b,
           ev_sconv_w, ev_sconv_b, ev_w_r, ev_b_r, ev_w_i, ev_b_i, ev_lam, ev_w_out,
           od_w_in, od_w_grp, od_scale, od_w_out, final_g):
    bn, seq, d = x.shape
    n_ctx = ctx.shape[1]
    w_conv = ev_conv_w.shape[-1]
    w_lru = ev_sconv_w.shape[-1]
    w_pool = od_scale.shape[-1]
    hd = w_lru // N_LRU_HEADS

    rows = jnp.swapaxes(x, 0, 1).reshape(seq * bn, d)
    ctx_rows = jnp.swapaxes(ctx, 0, 1).reshape(n_ctx * bn, d)

    n_cond = -(-(bn + 1) // 8) * 8
    cc = jnp.zeros((n_cond, d), F32).at[:bn].set(c).at[bn].set(c_ctx)
    mod = _modulation(cc, mod_w, mod_b)
    shift0, scale0, gate0 = (mod[0, :bn, k * d:(k + 1) * d] for k in range(3))
    shift1, scale1, gate1 = (mod[1, :bn, k * d:(k + 1) * d] for k in range(3))
    shift_c = jnp.broadcast_to(mod[0, bn:bn + 1, 0:d], (bn, d))
    scale_c = jnp.broadcast_to(mod[0, bn:bn + 1, d:2 * d], (bn, d))

    w_in = ev_w_in[0].astype(BF16)
    xb_col = 3 * w_conv
    (xb_ctx,) = _inproj(ctx_rows, norm_g[0], shift_c, scale_c, w_in[:, xb_col:xb_col + w_lru],
                        (("id", (0,), w_lru),), 512, "ctx_inproj")
    plan0 = (("glu", (0, w_conv), w_conv), ("silu", (2 * w_conv,), w_conv),
             ("id", (xb_col,), w_lru), ("silu", (xb_col + w_lru,), w_lru))
    u, sga, xb, sgb = _inproj(rows, norm_g[0], shift0, scale0, w_in, plan0, 512, "even_inproj")

    def gate_weights(k):
        return jnp.concatenate([ev_w_r[0, k], ev_w_i[0, k]], axis=-1).astype(BF16)

    scan = functools.partial(_lru_scan, sconv_w=ev_sconv_w[0], sconv_b=ev_sconv_b[0], steps=16)
    zero_state = jnp.zeros((bn, w_lru), F32)
    dirs = []
    for k, reverse in ((0, False), (1, True)):
        args = dict(wg=gate_weights(k), b_r=ev_b_r[0, k], b_i=ev_b_i[0, k], lam=ev_lam[0, k],
                    reverse=reverse)
        _, h_ctx = scan(xb_ctx, h0=zero_state, name=f"ctx_scan{k}", **args)
        hs, _ = scan(xb, h0=h_ctx, name=f"scan{k}", **args)
        dirs.append(hs)

    rows = _even_out(u, sga, sgb, dirs[0], dirs[1], rows, gate0, ev_conv_w[0], ev_conv_b[0],
                     ev_ln_g[0], ev_ln_b[0], ev_w_out[0].astype(BF16), steps=16)

    plan1 = (("id", (0,), w_pool), ("silu", (w_pool,), w_pool))
    u1, sg1 = _inproj(rows, norm_g[1], shift1, scale1, od_w_in[0].astype(BF16), plan1, 512,
                      "odd_inproj")
    out = _odd_out(u1.reshape(seq, bn, w_pool), sg1.reshape(seq, bn, w_pool),
                   rows.reshape(seq, bn, d), gate1, od_w_grp[0].astype(BF16), od_scale[0],
                   od_w_out[0].astype(BF16), final_g, nbb=16)
    return jnp.swapaxes(out, 0, 1)
```

```python
import functools

import jax
import jax.numpy as jnp
from jax import lax
from jax.experimental import pallas as pl
from jax.experimental.pallas import tpu as pltpu

F32 = jnp.float32
BF16 = jnp.bfloat16

RMS_EPS = 1e-6
LN_EPS = 1e-5
LRU_C = 8.0
N_LRU_HEADS = 8
CONV_WIDTH = 31
CONV_HALF = 15
SHORT_CONV_WIDTH = 4
GRID_W = 64
POOL_WINDOWS = (2, 4, 8, 16)

LANES = 128
SUBLANES = 8
VMEM_LIMIT_BYTES = 56 * 1024 * 1024


def _params(n_axes):
    return pltpu.CompilerParams(
        dimension_semantics=("arbitrary",) * n_axes,
        vmem_limit_bytes=VMEM_LIMIT_BYTES)


def _sigmoid(x):
    return 0.5 * jnp.tanh(0.5 * x) + 0.5


def _silu(x):
    return x * _sigmoid(x)


def _modulated_rmsnorm(x, g, shift, scale, nb):
    rows, d = x.shape
    ms = jnp.mean(x * x, axis=-1, keepdims=True)
    y = x * lax.rsqrt(ms + RMS_EPS) * g
    y = y.reshape(rows // nb, nb, d) * (1.0 + scale)[None] + shift[None]
    return y.reshape(rows, d)


def _mod_kernel(c_ref, w_ref, b_ref, o_ref):
    s = _silu(c_ref[...])
    o_ref[0] = jnp.dot(s, w_ref[0], preferred_element_type=F32,
                       precision=lax.Precision.HIGHEST) + b_ref[0]


def _modulation(cc, mod_w, mod_b):
    depth, d, d3 = mod_w.shape
    r = cc.shape[0]
    tn = 1024
    return pl.pallas_call(
        _mod_kernel,
        grid=(depth, d3 // tn),
        in_specs=[pl.BlockSpec((r, d), lambda l, j: (0, 0)),
                  pl.BlockSpec((1, d, tn), lambda l, j: (l, 0, j)),
                  pl.BlockSpec((1, 1, tn), lambda l, j: (l, 0, j))],
        out_specs=pl.BlockSpec((1, r, tn), lambda l, j: (l, 0, j)),
        out_shape=jax.ShapeDtypeStruct((depth, r, d3), F32),
        compiler_params=_params(2),
        name="modulation",
    )(cc, mod_w, mod_b.reshape(depth, 1, d3))


def _inproj_kernel(x_ref, g_ref, shift_ref, scale_ref, w_ref, *o_refs, plan, nb):
    hx = _modulated_rmsnorm(x_ref[...], g_ref[...], shift_ref[...], scale_ref[...], nb).astype(BF16)

    def proj(c0, width):
        return jnp.dot(hx, w_ref[:, c0:c0 + width], preferred_element_type=F32)

    for (kind, cols, width), o_ref in zip(plan, o_refs):
        if kind == "glu":
            o = proj(cols[0], width) * _sigmoid(proj(cols[1], width))
        elif kind == "silu":
            o = _silu(proj(cols[0], width))
        else:
            o = proj(cols[0], width)
        o_ref[...] = o.astype(o_ref.dtype)


def _inproj(rows, norm_g, shift, scale, w_bf16, plan, tm, name):
    n, d = rows.shape
    nb = shift.shape[0]
    kern = functools.partial(_inproj_kernel, plan=plan, nb=nb)
    const = lambda i: (0, 0)
    return pl.pallas_call(
        kern,
        grid=(n // tm,),
        in_specs=[pl.BlockSpec((tm, d), lambda i: (i, 0)),
                  pl.BlockSpec((1, d), const),
                  pl.BlockSpec((nb, d), const),
                  pl.BlockSpec((nb, d), const),
                  pl.BlockSpec(w_bf16.shape, const)],
        out_specs=[pl.BlockSpec((tm, width), lambda i: (i, 0)) for _, _, width in plan],
        out_shape=[jax.ShapeDtypeStruct((n, width), BF16) for _, _, width in plan],
        compiler_params=_params(1),
        name=name,
    )(rows, norm_g.reshape(1, d), shift, scale, w_bf16)


def _lru_kernel(prev_ref, cur_ref, next_ref, cw_ref, cb_ref, wg_ref, br_ref, bi_ref,
                lam_ref, h0_ref, hs_ref, ht_ref, xe_s, a_s, b_s, h_s, *, nb, reverse):
    i = pl.program_id(0)
    nc = pl.num_programs(0)
    c = nc - 1 - i if reverse else i
    r, w = cur_ref.shape
    steps = r // nb
    lead = (SHORT_CONV_WIDTH - 2) * nb

    @pl.when(i == 0)
    def _():
        h_s[...] = h0_ref[...]

    @pl.when(c > 0)
    def _():
        xe_s[0:lead] = prev_ref[...].astype(F32)

    @pl.when(c == 0)
    def _():
        xe_s[0:lead] = jnp.zeros((lead, w), F32)

    xe_s[lead:lead + r] = cur_ref[...].astype(F32)

    @pl.when(c < nc - 1)
    def _():
        xe_s[lead + r:lead + r + nb] = next_ref[...].astype(F32)

    @pl.when(c == nc - 1)
    def _():
        xe_s[lead + r:lead + r + nb] = jnp.zeros((nb, w), F32)

    neg = -lam_ref[...]
    softplus_neg = jnp.maximum(neg, 0.0) + jnp.log1p(jnp.exp(-jnp.abs(neg)))
    rate = LRU_C * softplus_neg

    hd = w // N_LRU_HEADS
    for h in range(N_LRU_HEADS):
        sl = slice(h * hd, (h + 1) * hd)
        u = cb_ref[:, sl] + cw_ref[0:1, sl] * xe_s[0:r, sl]
        for k in range(1, SHORT_CONV_WIDTH):
            u = u + cw_ref[k:k + 1, sl] * xe_s[k * nb:k * nb + r, sl]
        g = jnp.dot(u.astype(BF16), wg_ref[h], preferred_element_type=F32)
        rg = _sigmoid(g[:, :hd] + br_ref[:, sl])
        ig = _sigmoid(g[:, hd:] + bi_ref[:, sl])
        p = rg * rate[:, sl]
        a = jnp.exp(-p)
        a_s[:, sl] = a
        y = jnp.tanh(p) * (a * a + 1.0)
        b_s[:, sl] = jnp.exp2(0.5 * jnp.log2(y)) * (ig * u)

    def step(s, hcur):
        t = steps - 1 - s if reverse else s
        rows = pl.ds(pl.multiple_of(t * nb, nb), nb)
        hnew = a_s[rows, :] * hcur + b_s[rows, :]
        hs_ref[rows, :] = hnew
        return hnew

    hfin = lax.fori_loop(0, steps, step, h_s[...])
    h_s[...] = hfin
    ht_ref[...] = hfin


def _lru_scan(xb, sconv_w, sconv_b, wg, b_r, b_i, lam, h0, *, steps, reverse, name):
    n, w = xb.shape
    nb = h0.shape[0]
    r = steps * nb
    nc = n // r
    lead = (SHORT_CONV_WIDTH - 2) * nb
    chunk = (lambda i: nc - 1 - i) if reverse else (lambda i: i)
    const = lambda i: (0, 0)
    kern = functools.partial(_lru_kernel, nb=nb, reverse=reverse)
    return pl.pallas_call(
        kern,
        grid=(nc,),
        in_specs=[
            pl.BlockSpec((lead, w), lambda i: (jnp.maximum(chunk(i) * (r // lead) - 1, 0), 0)),
            pl.BlockSpec((r, w), lambda i: (chunk(i), 0)),
            pl.BlockSpec((nb, w), lambda i: (jnp.minimum((chunk(i) + 1) * (r // nb), n // nb - 1), 0)),
            pl.BlockSpec((SHORT_CONV_WIDTH, w), const),
            pl.BlockSpec((1, w), const),
            pl.BlockSpec(wg.shape, lambda i: (0, 0, 0)),
            pl.BlockSpec((1, w), const),
            pl.BlockSpec((1, w), const),
            pl.BlockSpec((1, w), const),
            pl.BlockSpec((nb, w), const),
        ],
        out_specs=[pl.BlockSpec((r, w), lambda i: (chunk(i), 0)),
                   pl.BlockSpec((nb, w), const)],
        out_shape=[jax.ShapeDtypeStruct((n, w), F32),
                   jax.ShapeDtypeStruct((nb, w), F32)],
        scratch_shapes=[pltpu.VMEM((lead + r + nb, w), F32),
                        pltpu.VMEM((r, w), F32),
                        pltpu.VMEM((r, w), F32),
                        pltpu.VMEM((nb, w), F32)],
        compiler_params=_params(1),
        name=name,
    )(xb, xb, xb, sconv_w, sconv_b.reshape(1, w), wg, b_r.reshape(1, w), b_i.reshape(1, w),
      lam.reshape(1, w), h0)


def _even_out_kernel(uprev_ref, ucur_ref, unext_ref, sga_ref, sgb_ref, hf_ref, hb_ref, x_ref,
                     gate_ref, cw_ref, cb_ref, lg_ref, lb_ref, wo_ref, o_ref, ue_s, v_s,
                     *, nb, row_tile):
    c = pl.program_id(0)
    nc = pl.num_programs(0)
    r, w = ucur_ref.shape
    halo = CONV_HALF * nb
    n_lane_tiles = w // LANES

    def copy_in(dst_rows, src_ref):
        for lt in range(n_lane_tiles):
            ue_s[lt, dst_rows] = src_ref[:, lt * LANES:(lt + 1) * LANES].astype(F32)

    def fill(dst_rows, src_ref, valid):
        pl.when(valid)(lambda: copy_in(dst_rows, src_ref))

        @pl.when(jnp.logical_not(valid))
        def _():
            for lt in range(n_lane_tiles):
                ue_s[lt, dst_rows] = jnp.zeros((r, LANES), F32)

    fill(slice(0, r), uprev_ref, c > 0)
    copy_in(slice(r, 2 * r), ucur_ref)
    fill(slice(2 * r, 3 * r), unext_ref, c < nc - 1)

    n_row_tiles = r // row_tile
    sub = row_tile // SUBLANES
    for lt in range(n_lane_tiles):
        ls = slice(lt * LANES, (lt + 1) * LANES)
        wk = [jnp.broadcast_to(cw_ref[k:k + 1, ls], (SUBLANES, LANES))[None]
              for k in range(CONV_WIDTH)]
        bias = jnp.broadcast_to(cb_ref[:, ls], (SUBLANES, LANES))[None]

        def body(j, carry, lt=lt, wk=wk, bias=bias):
            base = pl.multiple_of(j * row_tile, row_tile) + (r - halo)
            acc = jnp.broadcast_to(bias, (sub, SUBLANES, LANES))
            for k in range(CONV_WIDTH):
                tap = ue_s[lt, pl.ds(base + k * nb, row_tile), :].reshape(sub, SUBLANES, LANES)
                acc = acc + wk[k] * tap
            v_s[lt, pl.ds(pl.multiple_of(j * row_tile, row_tile), row_tile), :] = (
                acc.reshape(row_tile, LANES))
            return carry

        lax.fori_loop(0, n_row_tiles, body, 0)

    v = jnp.concatenate([v_s[lt] for lt in range(n_lane_tiles)], axis=-1)
    mu = jnp.mean(v, axis=-1, keepdims=True)
    vc = v - mu
    var = jnp.mean(vc * vc, axis=-1, keepdims=True)
    yn = vc * lax.rsqrt(var + LN_EPS) * lg_ref[...] + lb_ref[...]
    ya = (_silu(yn) * sga_ref[...].astype(F32)).astype(BF16)
    yb = ((hf_ref[...] + hb_ref[...]) * sgb_ref[...].astype(F32)).astype(BF16)
    out = jnp.dot(ya, wo_ref[0:w, :], preferred_element_type=F32)
    out = out + jnp.dot(yb, wo_ref[w:2 * w, :], preferred_element_type=F32)
    d = out.shape[-1]
    x = x_ref[...].reshape(r // nb, nb, d)
    o_ref[...] = (x + gate_ref[...][None] * out.reshape(r // nb, nb, d)).reshape(r, d)


def _even_out(u, sga, sgb, hf, hb, rows, gate, conv_w, conv_b, ln_g, ln_b, w_out_bf16, *, steps):
    n, w = u.shape
    d = rows.shape[1]
    nb = gate.shape[0]
    r = steps * nb
    nc = n // r
    assert steps >= CONV_HALF
    const = lambda i: (0, 0)
    tile = lambda i: (i, 0)
    kern = functools.partial(_even_out_kernel, nb=nb, row_tile=64)
    return pl.pallas_call(
        kern,
        grid=(nc,),
        in_specs=[
            pl.BlockSpec((r, w), lambda i: (jnp.maximum(i - 1, 0), 0)),
            pl.BlockSpec((r, w), tile),
            pl.BlockSpec((r, w), lambda i: (jnp.minimum(i + 1, nc - 1), 0)),
            pl.BlockSpec((r, w), tile),
            pl.BlockSpec((r, w), tile),
            pl.BlockSpec((r, w), tile),
            pl.BlockSpec((r, w), tile),
            pl.BlockSpec((r, d), tile),
            pl.BlockSpec((nb, d), const),
            pl.BlockSpec((CONV_WIDTH, w), const),
            pl.BlockSpec((1, w), const),
            pl.BlockSpec((1, w), const),
            pl.BlockSpec((1, w), const),
            pl.BlockSpec(w_out_bf16.shape, const),
        ],
        out_specs=pl.BlockSpec((r, d), tile),
        out_shape=jax.ShapeDtypeStruct((n, d), F32),
        scratch_shapes=[pltpu.VMEM((w // LANES, 3 * r, LANES), F32),
                        pltpu.VMEM((w // LANES, r, LANES), F32)],
        compiler_params=_params(1),
        name="even_out",
    )(u, u, u, sga, sgb, hf, hb, rows, gate, conv_w, conv_b.reshape(1, w), ln_g.reshape(1, w),
      ln_b.reshape(1, w), w_out_bf16)


def _odd_kernel(x_ref, ng_ref, shift_ref, scale_ref, gate_ref, win_ref, wgrp_ref, pscale_ref,
                wo_ref, fg_ref, o_ref, ue_s, inv_s, *, nbb):
    first = (pl.program_id(0) == 0) & (pl.program_id(1) == 0)
    gw, _, d = x_ref.shape
    n_groups = len(POOL_WINDOWS)
    wp = pscale_ref.shape[-1]
    gd = wp // n_groups
    r = gw * nbb
    pad = (max(POOL_WINDOWS) // 2) * nbb

    @pl.when(first)
    def _():
        t = lax.broadcasted_iota(jnp.int32, (r, LANES), 0) // nbb
        for gi, win in enumerate(POOL_WINDOWS):
            half = win // 2
            cnt = jnp.minimum(t + half, gw) - jnp.maximum(t - half, 0)
            inv_s[gi] = 1.0 / cnt.astype(F32)
        zeros = jnp.zeros((pad, gd), F32)
        for buf in range(ue_s.shape[0]):
            ue_s[buf, 0:pad] = zeros
            ue_s[buf, pad + r:pad + r + pad] = zeros

    x = x_ref[...]
    hx = _modulated_rmsnorm(x.reshape(r, d), ng_ref[...], shift_ref[...], scale_ref[...],
                            nbb).astype(BF16)
    acc = None
    for gi, win in enumerate(POOL_WINDOWS):
        gs = slice(gi * gd, (gi + 1) * gd)
        buf = gi % ue_s.shape[0]
        ue_s[buf, pad:pad + r] = jnp.dot(hx, win_ref[:, gs], preferred_element_type=F32)
        sgate = _silu(jnp.dot(hx, win_ref[:, wp + gi * gd:wp + (gi + 1) * gd],
                              preferred_element_type=F32))
        e = ue_s[buf]
        cur = e[0:r + 2 * pad - nbb] + e[nbb:r + 2 * pad]
        lo = 1
        span = 2
        while span < win:
            sh = (span // 2) * nbb
            cur = cur[0:cur.shape[0] - 2 * sh] + cur[2 * sh:]
            lo += span // 2
            span *= 2
        start = pad - lo * nbb
        inv = jnp.concatenate([inv_s[gi]] * (gd // LANES), axis=-1)
        dlt = (cur[start:start + r] * inv - ue_s[buf, pad:pad + r]).astype(BF16)
        y = jnp.dot(dlt, wgrp_ref[gi], preferred_element_type=F32)
        y = y * pscale_ref[:, gs] * sgate
        part = jnp.dot(y.astype(BF16), wo_ref[gs, :], preferred_element_type=F32)
        acc = part if acc is None else acc + part

    xo = x + gate_ref[...][None] * acc.reshape(gw, nbb, d)
    ms = jnp.mean(xo * xo, axis=-1, keepdims=True)
    o_ref[...] = xo * lax.rsqrt(ms + RMS_EPS) * fg_ref[...][None]


def _odd_layer(rows3, norm_g, shift, scale, gate, w_in_bf16, w_grp_bf16, pscale, w_out_bf16,
               final_g, *, nbb):
    seq, nb, d = rows3.shape
    n_groups = len(POOL_WINDOWS)
    wp = pscale.shape[-1]
    gd = wp // n_groups
    pad = (max(POOL_WINDOWS) // 2) * nbb
    r = GRID_W * nbb
    blk = lambda i, j: (i, j, 0)
    per_batch = lambda i, j: (j, 0)
    const = lambda i, j: (0, 0)
    resident = dict(pipeline_mode=pl.Buffered(1))
    kern = functools.partial(_odd_kernel, nbb=nbb)
    return pl.pallas_call(
        kern,
        grid=(seq // GRID_W, nb // nbb),
        in_specs=[
            pl.BlockSpec((GRID_W, nbb, d), blk),
            pl.BlockSpec((1, d), const),
            pl.BlockSpec((nbb, d), per_batch),
            pl.BlockSpec((nbb, d), per_batch),
            pl.BlockSpec((nbb, d), per_batch),
            pl.BlockSpec(w_in_bf16.shape, const, **resident),
            pl.BlockSpec(w_grp_bf16.shape, lambda i, j: (0, 0, 0), **resident),
            pl.BlockSpec((1, wp), const),
            pl.BlockSpec(w_out_bf16.shape, const, **resident),
            pl.BlockSpec((1, d), const),
        ],
        out_specs=pl.BlockSpec((GRID_W, nbb, d), blk),
        out_shape=jax.ShapeDtypeStruct((seq, nb, d), F32),
        scratch_shapes=[pltpu.VMEM((2, r + 2 * pad, gd), F32),
                        pltpu.VMEM((n_groups, r, LANES), F32)],
        compiler_params=_params(2),
        name="odd_layer",
    )(rows3, norm_g.reshape(1, d), shift, scale, gate, w_in_bf16, w_grp_bf16,
      pscale.reshape(1, wp), w_out_bf16, final_g.reshape(1, d))


def kernel(x, c, ctx, c_ctx, norm_g, mod_w, mod_b, ev_w_in, ev_conv_w, ev_conv_b, ev_ln_g, ev_ln_b,
           ev_sconv_w, ev_sconv_b, ev_w_r, ev_b_r, ev_w_i, ev_b_i, ev_lam, ev_w_out,
           od_w_in, od_w_grp, od_scale, od_w_out, final_g):
    bn, seq, d = x.shape
    n_ctx = ctx.shape[1]
    w_conv = ev_conv_w.shape[-1]
    w_lru = ev_sconv_w.shape[-1]

    rows = jnp.swapaxes(x, 0, 1).reshape(seq * bn, d)
    ctx_rows = jnp.swapaxes(ctx, 0, 1).reshape(n_ctx * bn, d)

    n_cond = -(-(bn + 1) // SUBLANES) * SUBLANES
    cc = jnp.zeros((n_cond, d), F32).at[:bn].set(c).at[bn].set(c_ctx)
    mod = _modulation(cc, mod_w, mod_b)
    shift0, scale0, gate0 = (mod[0, :bn, k * d:(k + 1) * d] for k in range(3))
    shift1, scale1, gate1 = (mod[1, :bn, k * d:(k + 1) * d] for k in range(3))
    shift_c = jnp.broadcast_to(mod[0, bn:bn + 1, 0:d], (bn, d))
    scale_c = jnp.broadcast_to(mod[0, bn:bn + 1, d:2 * d], (bn, d))

    w_in = ev_w_in[0].astype(BF16)
    xb_col = 3 * w_conv
    (xb_ctx,) = _inproj(ctx_rows, norm_g[0], shift_c, scale_c, w_in[:, xb_col:xb_col + w_lru],
                        (("id", (0,), w_lru),), 512, "ctx_inproj")
    plan0 = (("glu", (0, w_conv), w_conv), ("silu", (2 * w_conv,), w_conv),
             ("id", (xb_col,), w_lru), ("silu", (xb_col + w_lru,), w_lru))
    u, sga, xb, sgb = _inproj(rows, norm_g[0], shift0, scale0, w_in, plan0, 512, "even_inproj")

    def gate_weights(k):
        return jnp.concatenate([ev_w_r[0, k], ev_w_i[0, k]], axis=-1).astype(BF16)

    scan = functools.partial(_lru_scan, sconv_w=ev_sconv_w[0], sconv_b=ev_sconv_b[0], steps=16)
    zero_state = jnp.zeros((bn, w_lru), F32)
    dirs = []
    for k, reverse in ((0, False), (1, True)):
        args = dict(wg=gate_weights(k), b_r=ev_b_r[0, k], b_i=ev_b_i[0, k], lam=ev_lam[0, k],
                    reverse=reverse)
        _, h_ctx = scan(xb_ctx, h0=zero_state, name=f"ctx_scan{k}", **args)
        hs, _ = scan(xb, h0=h_ctx, name=f"scan{k}", **args)
        dirs.append(hs)

    rows = _even_out(u, sga, sgb, dirs[0], dirs[1], rows, gate0, ev_conv_w[0], ev_conv_b[0],
                     ev_ln_g[0], ev_ln_b[0], ev_w_out[0].astype(BF16), steps=16)

    out = _odd_layer(rows.reshape(seq, bn, d), norm_g[1], shift1, scale1, gate1,
                     od_w_in[0].astype(BF16), od_w_grp[0].astype(BF16), od_scale[0],
                     od_w_out[0].astype(BF16), final_g, nbb=16)
    return jnp.swapaxes(out, 0, 1)
```

```python
import functools

import jax
import jax.numpy as jnp
from jax import lax
from jax.experimental import pallas as pl
from jax.experimental.pallas import tpu as pltpu

F32 = jnp.float32
BF16 = jnp.bfloat16

RMS_EPS = 1e-6
LN_EPS = 1e-5
LRU_C = 8.0
N_LRU_HEADS = 8
CONV_WIDTH = 31
CONV_HALF = 15
SHORT_CONV_WIDTH = 4
SHORT_CONV_LEAD = 2
GRID_W = 64
POOL_WINDOWS = (2, 4, 8, 16)

LANES = 128
SUBLANES = 8
VMEM_LIMIT_BYTES = 56 * 1024 * 1024


def _params(n_axes):
    return pltpu.CompilerParams(
        dimension_semantics=("arbitrary",) * n_axes,
        vmem_limit_bytes=VMEM_LIMIT_BYTES)


def _resident(shape, index_map):
    return pl.BlockSpec(shape, index_map, pipeline_mode=pl.Buffered(1))


def _sigmoid(x):
    return 0.5 * jnp.tanh(0.5 * x) + 0.5


def _silu(x):
    return x * _sigmoid(x)


def _modulated_rmsnorm(x, g, shift, scale, nb):
    rows, d = x.shape
    ms = jnp.mean(x * x, axis=-1, keepdims=True)
    y = x * lax.rsqrt(ms + RMS_EPS) * g
    y = y.reshape(rows // nb, nb, d) * (1.0 + scale)[None] + shift[None]
    return y.reshape(rows, d)


def _mod_kernel(c_ref, w_ref, b_ref, o_ref):
    s = _silu(c_ref[...])
    o_ref[0] = jnp.dot(s, w_ref[0], preferred_element_type=F32,
                       precision=lax.Precision.HIGHEST) + b_ref[0]


def _modulation(cc, mod_w, mod_b):
    depth, d, d3 = mod_w.shape
    r = cc.shape[0]
    tn = 1024
    return pl.pallas_call(
        _mod_kernel,
        grid=(depth, d3 // tn),
        in_specs=[pl.BlockSpec((r, d), lambda l, j: (0, 0)),
                  pl.BlockSpec((1, d, tn), lambda l, j: (l, 0, j)),
                  pl.BlockSpec((1, 1, tn), lambda l, j: (l, 0, j))],
        out_specs=pl.BlockSpec((1, r, tn), lambda l, j: (l, 0, j)),
        out_shape=jax.ShapeDtypeStruct((depth, r, d3), F32),
        compiler_params=_params(2),
        name="modulation",
    )(cc, mod_w, mod_b.reshape(depth, 1, d3))


def _lru_rate(lam):
    neg = -lam
    return LRU_C * (jnp.maximum(neg, 0.0) + jnp.log1p(jnp.exp(-jnp.abs(neg))))


def _lru_coeffs(xe_s, r, nb, cw_ref, cb_ref, wg_ref, br_ref, bi_ref, lam_ref, a_s, b_s,
                between_heads=None):
    w = a_s.shape[-1]
    rate = _lru_rate(lam_ref[...])
    hd = w // N_LRU_HEADS
    for h in range(N_LRU_HEADS):
        if between_heads is not None and h > 0:
            between_heads(h - 1)
        sl = slice(h * hd, (h + 1) * hd)
        u = cb_ref[:, sl] + cw_ref[0:1, sl] * xe_s[0:r, sl]
        for k in range(1, SHORT_CONV_WIDTH):
            u = u + cw_ref[k:k + 1, sl] * xe_s[k * nb:k * nb + r, sl]
        g = jnp.dot(u.astype(BF16), wg_ref[h], preferred_element_type=F32)
        rg = _sigmoid(g[:, :hd] + br_ref[:, sl])
        ig = _sigmoid(g[:, hd:] + bi_ref[:, sl])
        p = rg * rate[:, sl]
        a = jnp.exp(-p)
        a_s[:, sl] = a
        y = jnp.tanh(p) * (a * a + 1.0)
        b_s[:, sl] = jnp.exp2(0.5 * jnp.log2(y)) * (ig * u)
    if between_heads is not None:
        between_heads(N_LRU_HEADS - 1)


def _lru_steps(a_s, b_s, h, hs_ref, steps, nb, reverse):
    def step(s, hcur):
        t = steps - 1 - s if reverse else s
        rows = pl.ds(pl.multiple_of(t * nb, nb), nb)
        hnew = a_s[rows, :] * hcur + b_s[rows, :]
        hs_ref[rows, :] = hnew
        return hnew

    return lax.fori_loop(0, steps, step, h)


def _lru_kernel(prev_ref, cur_ref, next_ref, cw_ref, cb_ref, wg_ref, br_ref, bi_ref,
                lam_ref, h0_ref, hs_ref, ht_ref, xe_s, a_s, b_s, h_s, *, nb, reverse):
    i = pl.program_id(0)
    nc = pl.num_programs(0)
    c = nc - 1 - i if reverse else i
    r, w = cur_ref.shape
    lead = SHORT_CONV_LEAD * nb

    @pl.when(i == 0)
    def _():
        h_s[...] = h0_ref[...]

    @pl.when(c > 0)
    def _():
        xe_s[0:lead] = prev_ref[...].astype(F32)

    @pl.when(c == 0)
    def _():
        xe_s[0:lead] = jnp.zeros((lead, w), F32)

    xe_s[lead:lead + r] = cur_ref[...].astype(F32)

    @pl.when(c < nc - 1)
    def _():
        xe_s[lead + r:lead + r + nb] = next_ref[...].astype(F32)

    @pl.when(c == nc - 1)
    def _():
        xe_s[lead + r:lead + r + nb] = jnp.zeros((nb, w), F32)

    _lru_coeffs(xe_s, r, nb, cw_ref, cb_ref, wg_ref, br_ref, bi_ref, lam_ref, a_s, b_s)
    hfin = _lru_steps(a_s, b_s, h_s[...], hs_ref, r // nb, nb, reverse)
    h_s[...] = hfin
    ht_ref[...] = hfin


def _lru_scan(xb, sconv_w, sconv_b, wg, b_r, b_i, lam, h0, *, steps, reverse, name):
    n, w = xb.shape
    nb = h0.shape[0]
    r = steps * nb
    nc = n // r
    lead = SHORT_CONV_LEAD * nb
    chunk = (lambda i: nc - 1 - i) if reverse else (lambda i: i)
    const = lambda i: (0, 0)
    kern = functools.partial(_lru_kernel, nb=nb, reverse=reverse)
    return pl.pallas_call(
        kern,
        grid=(nc,),
        in_specs=[
            pl.BlockSpec((lead, w), lambda i: (jnp.maximum(chunk(i) * (r // lead) - 1, 0), 0)),
            pl.BlockSpec((r, w), lambda i: (chunk(i), 0)),
            pl.BlockSpec((nb, w), lambda i: (jnp.minimum((chunk(i) + 1) * (r // nb), n // nb - 1), 0)),
            pl.BlockSpec((SHORT_CONV_WIDTH, w), const),
            pl.BlockSpec((1, w), const),
            pl.BlockSpec(wg.shape, lambda i: (0, 0, 0)),
            pl.BlockSpec((1, w), const),
            pl.BlockSpec((1, w), const),
            pl.BlockSpec((1, w), const),
            pl.BlockSpec((nb, w), const),
        ],
        out_specs=[pl.BlockSpec((r, w), lambda i: (chunk(i), 0)),
                   pl.BlockSpec((nb, w), const)],
        out_shape=[jax.ShapeDtypeStruct((n, w), F32),
                   jax.ShapeDtypeStruct((nb, w), F32)],
        scratch_shapes=[pltpu.VMEM((lead + r + nb, w), F32),
                        pltpu.VMEM((r, w), F32),
                        pltpu.VMEM((r, w), F32),
                        pltpu.VMEM((nb, w), F32)],
        compiler_params=_params(1),
        name=name,
    )(xb, xb, xb, sconv_w, sconv_b.reshape(1, w), wg, b_r.reshape(1, w), b_i.reshape(1, w),
      lam.reshape(1, w), h0)


def _ctx_inproj_kernel(x_ref, g_ref, shift_ref, scale_ref, w_ref, o_ref, *, nb):
    hx = _modulated_rmsnorm(x_ref[...], g_ref[...], shift_ref[...], scale_ref[...], nb).astype(BF16)
    o_ref[...] = jnp.dot(hx, w_ref[...], preferred_element_type=F32).astype(o_ref.dtype)


def _ctx_inproj(rows, norm_g, shift, scale, w_bf16, tm):
    n, d = rows.shape
    nb = shift.shape[0]
    width = w_bf16.shape[1]
    const = lambda i: (0, 0)
    return pl.pallas_call(
        functools.partial(_ctx_inproj_kernel, nb=nb),
        grid=(n // tm,),
        in_specs=[pl.BlockSpec((tm, d), lambda i: (i, 0)),
                  pl.BlockSpec((1, d), const),
                  pl.BlockSpec((nb, d), const),
                  pl.BlockSpec((nb, d), const),
                  _resident(w_bf16.shape, const)],
        out_specs=pl.BlockSpec((tm, width), lambda i: (i, 0)),
        out_shape=jax.ShapeDtypeStruct((n, width), BF16),
        compiler_params=_params(1),
        name="ctx_inproj",
    )(rows, norm_g.reshape(1, d), shift, scale, w_bf16)


def _even_in_kernel(x_ref, g_ref, shift_ref, scale_ref, w_ref, cw_ref, cb_ref, wg_ref, br_ref,
                    bi_ref, lam_ref, h0_ref, u_ref, sga_ref, xb_ref, sgb_ref, hs_ref, ht_ref,
                    xe_s, a_s, b_s, h_s, *, nb):
    c = pl.program_id(0)
    nc = pl.num_programs(0) - 1
    r, _ = x_ref.shape
    w = u_ref.shape[-1]
    lead = SHORT_CONV_LEAD * nb

    @pl.when(c == 0)
    def _():
        h_s[...] = h0_ref[...]
        xe_s[...] = jnp.zeros(xe_s.shape, F32)

    xe_s[0:lead + r] = xe_s[r:lead + 2 * r]

    hx = _modulated_rmsnorm(x_ref[...], g_ref[...], shift_ref[...], scale_ref[...], nb).astype(BF16)

    def proj(group, cols=slice(0, w)):
        return jnp.dot(hx, w_ref[:, group * w + cols.start:group * w + cols.stop],
                       preferred_element_type=F32)

    xb = proj(3)
    xb_ref[...] = xb.astype(xb_ref.dtype)
    xe_s[lead + r:lead + 2 * r] = xb * (c < nc).astype(F32)

    q = w // (N_LRU_HEADS // 2)

    def proj_slice(i):
        if i < N_LRU_HEADS // 2:
            cols = slice(i * q, (i + 1) * q)
            u_ref[:, cols] = (proj(0, cols) * _sigmoid(proj(1, cols))).astype(u_ref.dtype)
        else:
            j = i - N_LRU_HEADS // 2
            o_ref, group = (sga_ref, 2) if j < N_LRU_HEADS // 4 else (sgb_ref, 4)
            cols = slice((j % (N_LRU_HEADS // 4)) * 2 * q, (j % (N_LRU_HEADS // 4) + 1) * 2 * q)
            o_ref[:, cols] = _silu(proj(group, cols)).astype(o_ref.dtype)

    _lru_coeffs(xe_s, r, nb, cw_ref, cb_ref, wg_ref, br_ref, bi_ref, lam_ref, a_s, b_s,
                between_heads=proj_slice)

    h_prev = h_s[...]
    hfin = _lru_steps(a_s, b_s, h_prev, hs_ref, r // nb, nb, reverse=False)
    hfin = jnp.where(c > 0, hfin, h_prev)
    h_s[...] = hfin
    ht_ref[...] = hfin


def _even_in(rows, norm_g, shift, scale, w_in_bf16, sconv_w, sconv_b, wg, b_r, b_i, lam, h0,
             *, steps):
    n, d = rows.shape
    w = sconv_w.shape[-1]
    nb = shift.shape[0]
    r = steps * nb
    nc = n // r
    lead = SHORT_CONV_LEAD * nb
    assert w_in_bf16.shape == (d, 5 * w)
    const = lambda i: (0, 0)
    cur = lambda i: (jnp.minimum(i, nc - 1), 0)
    delayed = lambda i: (jnp.maximum(i - 1, 0), 0)
    kern = functools.partial(_even_in_kernel, nb=nb)
    act = jax.ShapeDtypeStruct((n, w), BF16)
    return pl.pallas_call(
        kern,
        grid=(nc + 1,),
        in_specs=[pl.BlockSpec((r, d), cur),
                  pl.BlockSpec((1, d), const),
                  pl.BlockSpec((nb, d), const),
                  pl.BlockSpec((nb, d), const),
                  _resident(w_in_bf16.shape, const),
                  pl.BlockSpec((SHORT_CONV_WIDTH, w), const),
                  pl.BlockSpec((1, w), const),
                  _resident(wg.shape, lambda i: (0, 0, 0)),
                  pl.BlockSpec((1, w), const),
                  pl.BlockSpec((1, w), const),
                  pl.BlockSpec((1, w), const),
                  pl.BlockSpec((nb, w), const)],
        out_specs=[pl.BlockSpec((r, w), cur),
                   pl.BlockSpec((r, w), cur),
                   pl.BlockSpec((r, w), cur),
                   pl.BlockSpec((r, w), cur),
                   pl.BlockSpec((r, w), delayed),
                   pl.BlockSpec((nb, w), const)],
        out_shape=[act, act, act, act,
                   jax.ShapeDtypeStruct((n, w), F32),
                   jax.ShapeDtypeStruct((nb, w), F32)],
        scratch_shapes=[pltpu.VMEM((lead + 2 * r, w), F32),
                        pltpu.VMEM((r, w), F32),
                        pltpu.VMEM((r, w), F32),
                        pltpu.VMEM((nb, w), F32)],
        compiler_params=_params(1),
        name="even_in",
    )(rows, norm_g.reshape(1, d), shift, scale, w_in_bf16, sconv_w, sconv_b.reshape(1, w), wg,
      b_r.reshape(1, w), b_i.reshape(1, w), lam.reshape(1, w), h0)


def _even_out_kernel(uprev_ref, ucur_ref, unext_ref, sga_ref, sgb_ref, hf_ref, hb_ref, x_ref,
                     gate_ref, cw_ref, cb_ref, lg_ref, lb_ref, wo_ref, o_ref, ue_s, v_s, wrep_s,
                     *, nb, row_tile):
    c = pl.program_id(0)
    nc = pl.num_programs(0)
    r, w = ucur_ref.shape
    halo = CONV_HALF * nb
    n_lane_tiles = w // LANES

    @pl.when(c == 0)
    def _():
        for k in range(CONV_WIDTH):
            wrep_s[k] = jnp.broadcast_to(cw_ref[k:k + 1, :], (SUBLANES, w))

    def copy_in(dst_rows, src_ref):
        for lt in range(n_lane_tiles):
            ue_s[lt, dst_rows] = src_ref[:, lt * LANES:(lt + 1) * LANES].astype(F32)

    def fill(dst_rows, src_ref, valid):
        pl.when(valid)(lambda: copy_in(dst_rows, src_ref))

        @pl.when(jnp.logical_not(valid))
        def _():
            for lt in range(n_lane_tiles):
                ue_s[lt, dst_rows] = jnp.zeros((r, LANES), F32)

    fill(slice(0, r), uprev_ref, c > 0)
    copy_in(slice(r, 2 * r), ucur_ref)
    fill(slice(2 * r, 3 * r), unext_ref, c < nc - 1)

    n_row_tiles = r // row_tile
    sub = row_tile // SUBLANES
    for lt in range(n_lane_tiles):
        ls = slice(lt * LANES, (lt + 1) * LANES)
        bias = jnp.broadcast_to(cb_ref[:, ls], (SUBLANES, LANES))[None]

        def body(j, carry, lt=lt, ls=ls, bias=bias):
            base = pl.multiple_of(j * row_tile, row_tile) + (r - halo)
            def term(k):
                tap = ue_s[lt, pl.ds(base + k * nb, row_tile), :].reshape(sub, SUBLANES, LANES)
                return wrep_s[k, :, ls][None] * tap

            acc = jnp.broadcast_to(bias, (sub, SUBLANES, LANES)) + term(0)
            for k in range(1, CONV_WIDTH, 2):
                acc = acc + (term(k) + term(k + 1))
            v_s[lt, pl.ds(pl.multiple_of(j * row_tile, row_tile), row_tile), :] = (
                acc.reshape(row_tile, LANES))
            return carry

        lax.fori_loop(0, n_row_tiles, body, 0)

    v = jnp.concatenate([v_s[lt] for lt in range(n_lane_tiles)], axis=-1)
    mu = jnp.mean(v, axis=-1, keepdims=True)
    vc = v - mu
    var = jnp.mean(vc * vc, axis=-1, keepdims=True)
    yn = vc * lax.rsqrt(var + LN_EPS) * lg_ref[...] + lb_ref[...]
    ya = (_silu(yn) * sga_ref[...].astype(F32)).astype(BF16)
    yb = ((hf_ref[...] + hb_ref[...]) * sgb_ref[...].astype(F32)).astype(BF16)
    out = jnp.dot(ya, wo_ref[0:w, :], preferred_element_type=F32)
    out = out + jnp.dot(yb, wo_ref[w:2 * w, :], preferred_element_type=F32)
    d = out.shape[-1]
    x = x_ref[...].reshape(r // nb, nb, d)
    o_ref[...] = (x + gate_ref[...][None] * out.reshape(r // nb, nb, d)).reshape(r, d)


def _even_out(u, sga, sgb, hf, hb, rows, gate, conv_w, conv_b, ln_g, ln_b, w_out_bf16, *, steps):
    n, w = u.shape
    d = rows.shape[1]
    nb = gate.shape[0]
    r = steps * nb
    nc = n // r
    assert steps >= CONV_HALF
    const = lambda i: (0, 0)
    tile = lambda i: (i, 0)
    kern = functools.partial(_even_out_kernel, nb=nb, row_tile=128)
    return pl.pallas_call(
        kern,
        grid=(nc,),
        in_specs=[
            pl.BlockSpec((r, w), lambda i: (jnp.maximum(i - 1, 0), 0)),
            pl.BlockSpec((r, w), tile),
            pl.BlockSpec((r, w), lambda i: (jnp.minimum(i + 1, nc - 1), 0)),
            pl.BlockSpec((r, w), tile),
            pl.BlockSpec((r, w), tile),
            pl.BlockSpec((r, w), tile),
            pl.BlockSpec((r, w), tile),
            pl.BlockSpec((r, d), tile),
            pl.BlockSpec((nb, d), const),
            pl.BlockSpec((CONV_WIDTH, w), const),
            pl.BlockSpec((1, w), const),
            pl.BlockSpec((1, w), const),
            pl.BlockSpec((1, w), const),
            _resident(w_out_bf16.shape, const),
        ],
        out_specs=pl.BlockSpec((r, d), tile),
        out_shape=jax.ShapeDtypeStruct((n, d), F32),
        scratch_shapes=[pltpu.VMEM((w // LANES, 3 * r, LANES), F32),
                        pltpu.VMEM((w // LANES, r, LANES), F32),
                        pltpu.VMEM((CONV_WIDTH, SUBLANES, w), F32)],
        compiler_params=_params(1),
        name="even_out",
    )(u, u, u, sga, sgb, hf, hb, rows, gate, conv_w, conv_b.reshape(1, w), ln_g.reshape(1, w),
      ln_b.reshape(1, w), w_out_bf16)


def _odd_kernel(x_ref, ng_ref, shift_ref, scale_ref, gate_ref, win_ref, wgrp_ref, pscale_ref,
                wo_ref, fg_ref, o_ref, ue_s, inv_s, *, nbb):
    first = (pl.program_id(0) == 0) & (pl.program_id(1) == 0)
    gw, _, d = x_ref.shape
    n_groups = len(POOL_WINDOWS)
    wp = pscale_ref.shape[-1]
    gd = wp // n_groups
    r = gw * nbb
    pad = (max(POOL_WINDOWS) // 2) * nbb

    @pl.when(first)
    def _():
        t = lax.broadcasted_iota(jnp.int32, (r, LANES), 0) // nbb
        for gi, win in enumerate(POOL_WINDOWS):
            half = win // 2
            cnt = jnp.minimum(t + half, gw) - jnp.maximum(t - half, 0)
            inv_s[gi] = 1.0 / cnt.astype(F32)
        zeros = jnp.zeros((pad, gd), F32)
        for buf in range(ue_s.shape[0]):
            ue_s[buf, 0:pad] = zeros
            ue_s[buf, pad + r:pad + r + pad] = zeros

    x = x_ref[...]
    hx = _modulated_rmsnorm(x.reshape(r, d), ng_ref[...], shift_ref[...], scale_ref[...],
                            nbb).astype(BF16)
    acc = None
    for gi, win in enumerate(POOL_WINDOWS):
        gs = slice(gi * gd, (gi + 1) * gd)
        buf = gi % ue_s.shape[0]
        ue_s[buf, pad:pad + r] = jnp.dot(hx, win_ref[:, gs], preferred_element_type=F32)
        sgate = _silu(jnp.dot(hx, win_ref[:, wp + gi * gd:wp + (gi + 1) * gd],
                              preferred_element_type=F32))
        e = ue_s[buf]
        cur = e[0:r + 2 * pad - nbb] + e[nbb:r + 2 * pad]
        lo = 1
        span = 2
        while span < win:
            sh = (span // 2) * nbb
            cur = cur[0:cur.shape[0] - 2 * sh] + cur[2 * sh:]
            lo += span // 2
            span *= 2
        start = pad - lo * nbb
        inv = jnp.concatenate([inv_s[gi]] * (gd // LANES), axis=-1)
        dlt = (cur[start:start + r] * inv - ue_s[buf, pad:pad + r]).astype(BF16)
        y = jnp.dot(dlt, wgrp_ref[gi], preferred_element_type=F32)
        y = y * pscale_ref[:, gs] * sgate
        part = jnp.dot(y.astype(BF16), wo_ref[gs, :], preferred_element_type=F32)
        acc = part if acc is None else acc + part

    xo = x + gate_ref[...][None] * acc.reshape(gw, nbb, d)
    ms = jnp.mean(xo * xo, axis=-1, keepdims=True)
    o_ref[...] = xo * lax.rsqrt(ms + RMS_EPS) * fg_ref[...][None]


def _odd_layer(rows3, norm_g, shift, scale, gate, w_in_bf16, w_grp_bf16, pscale, w_out_bf16,
               final_g, *, nbb):
    seq, nb, d = rows3.shape
    n_groups = len(POOL_WINDOWS)
    wp = pscale.shape[-1]
    gd = wp // n_groups
    pad = (max(POOL_WINDOWS) // 2) * nbb
    r = GRID_W * nbb
    blk = lambda i, j: (i, j, 0)
    per_batch = lambda i, j: (j, 0)
    const = lambda i, j: (0, 0)
    kern = functools.partial(_odd_kernel, nbb=nbb)
    return pl.pallas_call(
        kern,
        grid=(seq // GRID_W, nb // nbb),
        in_specs=[
            pl.BlockSpec((GRID_W, nbb, d), blk),
            pl.BlockSpec((1, d), const),
            pl.BlockSpec((nbb, d), per_batch),
            pl.BlockSpec((nbb, d), per_batch),
            pl.BlockSpec((nbb, d), per_batch),
            _resident(w_in_bf16.shape, const),
            _resident(w_grp_bf16.shape, lambda i, j: (0, 0, 0)),
            pl.BlockSpec((1, wp), const),
            _resident(w_out_bf16.shape, const),
            pl.BlockSpec((1, d), const),
        ],
        out_specs=pl.BlockSpec((GRID_W, nbb, d), blk),
        out_shape=jax.ShapeDtypeStruct((seq, nb, d), F32),
        scratch_shapes=[pltpu.VMEM((2, r + 2 * pad, gd), F32),
                        pltpu.VMEM((n_groups, r, LANES), F32)],
        compiler_params=_params(2),
        name="odd_layer",
    )(rows3, norm_g.reshape(1, d), shift, scale, gate, w_in_bf16, w_grp_bf16,
      pscale.reshape(1, wp), w_out_bf16, final_g.reshape(1, d))


def kernel(x, c, ctx, c_ctx, norm_g, mod_w, mod_b, ev_w_in, ev_conv_w, ev_conv_b, ev_ln_g, ev_ln_b,
           ev_sconv_w, ev_sconv_b, ev_w_r, ev_b_r, ev_w_i, ev_b_i, ev_lam, ev_w_out,
           od_w_in, od_w_grp, od_scale, od_w_out, final_g):
    bn, seq, d = x.shape
    n_ctx = ctx.shape[1]
    w_conv = ev_conv_w.shape[-1]
    w_lru = ev_sconv_w.shape[-1]
    assert w_conv == w_lru

    rows = jnp.swapaxes(x, 0, 1).reshape(seq * bn, d)
    ctx_rows = jnp.swapaxes(ctx, 0, 1).reshape(n_ctx * bn, d)

    n_cond = -(-(bn + 1) // SUBLANES) * SUBLANES
    cc = jnp.zeros((n_cond, d), F32).at[:bn].set(c).at[bn].set(c_ctx)
    mod = _modulation(cc, mod_w, mod_b)
    shift0, scale0, gate0 = (mod[0, :bn, k * d:(k + 1) * d] for k in range(3))
    shift1, scale1, gate1 = (mod[1, :bn, k * d:(k + 1) * d] for k in range(3))
    shift_c = jnp.broadcast_to(mod[0, bn:bn + 1, 0:d], (bn, d))
    scale_c = jnp.broadcast_to(mod[0, bn:bn + 1, d:2 * d], (bn, d))

    w_in = ev_w_in[0].astype(BF16)
    xb_col = 3 * w_conv
    xb_ctx = _ctx_inproj(ctx_rows, norm_g[0], shift_c, scale_c, w_in[:, xb_col:xb_col + w_lru], 512)

    def lru_params(k):
        wg = jnp.concatenate([ev_w_r[0, k], ev_w_i[0, k]], axis=-1).astype(BF16)
        return dict(sconv_w=ev_sconv_w[0], sconv_b=ev_sconv_b[0], wg=wg, b_r=ev_b_r[0, k],
                    b_i=ev_b_i[0, k], lam=ev_lam[0, k])

    zero_state = jnp.zeros((bn, w_lru), F32)
    _, h_ctx_f = _lru_scan(xb_ctx, h0=zero_state, steps=16, reverse=False, name="ctx_scan0",
                           **lru_params(0))
    _, h_ctx_b = _lru_scan(xb_ctx, h0=zero_state, steps=16, reverse=True, name="ctx_scan1",
                           **lru_params(1))

    u, sga, xb, sgb, hf, _ = _even_in(rows, norm_g[0], shift0, scale0, w_in, h0=h_ctx_f, steps=16,
                                      **lru_params(0))
    hb, _ = _lru_scan(xb, h0=h_ctx_b, steps=16, reverse=True, name="scan1", **lru_params(1))

    rows = _even_out(u, sga, sgb, hf, hb, rows, gate0, ev_conv_w[0], ev_conv_b[0],
                     ev_ln_g[0], ev_ln_b[0], ev_w_out[0].astype(BF16), steps=16)

    out = _odd_layer(rows.reshape(seq, bn, d), norm_g[1], shift1, scale1, gate1,
                     od_w_in[0].astype(BF16), od_w_grp[0].astype(BF16), od_scale[0],
                     od_w_out[0].astype(BF16), final_g, nbb=16)
    return jnp.swapaxes(out, 0, 1)
```

```python
import functools

import jax
import jax.numpy as jnp
from jax import lax
from jax.experimental import pallas as pl
from jax.experimental.pallas import tpu as pltpu

F32 = jnp.float32
BF16 = jnp.bfloat16

RMS_EPS = 1e-6
LN_EPS = 1e-5
LRU_C = 8.0
N_LRU_HEADS = 8
CONV_WIDTH = 31
CONV_HALF = 15
SHORT_CONV_WIDTH = 4
SHORT_CONV_LEAD = 2
GRID_W = 64
POOL_WINDOWS = (2, 4, 8, 16)

LANES = 128
SUBLANES = 8
VMEM_LIMIT_BYTES = 56 * 1024 * 1024


def _params(n_axes):
    return pltpu.CompilerParams(
        dimension_semantics=("arbitrary",) * n_axes,
        vmem_limit_bytes=VMEM_LIMIT_BYTES)


def _resident(shape, index_map):
    return pl.BlockSpec(shape, index_map, pipeline_mode=pl.Buffered(1))


def _sigmoid(x):
    return 0.5 * jnp.tanh(0.5 * x) + 0.5


def _silu(x):
    return x * _sigmoid(x)


def _modulated_rmsnorm(x, g, shift, scale, nb):
    rows, d = x.shape
    ms = jnp.mean(x * x, axis=-1, keepdims=True)
    y = x * lax.rsqrt(ms + RMS_EPS) * g
    y = y.reshape(rows // nb, nb, d) * (1.0 + scale)[None] + shift[None]
    return y.reshape(rows, d)


def _mod_kernel(c_ref, w_ref, b_ref, o_ref):
    s = _silu(c_ref[...])
    o_ref[0] = jnp.dot(s, w_ref[0], preferred_element_type=F32,
                       precision=lax.Precision.HIGHEST) + b_ref[0]


def _modulation(cc, mod_w, mod_b):
    depth, d, d3 = mod_w.shape
    r = cc.shape[0]
    tn = 1024
    return pl.pallas_call(
        _mod_kernel,
        grid=(depth, d3 // tn),
        in_specs=[pl.BlockSpec((r, d), lambda l, j: (0, 0)),
                  pl.BlockSpec((1, d, tn), lambda l, j: (l, 0, j)),
                  pl.BlockSpec((1, 1, tn), lambda l, j: (l, 0, j))],
        out_specs=pl.BlockSpec((1, r, tn), lambda l, j: (l, 0, j)),
        out_shape=jax.ShapeDtypeStruct((depth, r, d3), F32),
        compiler_params=_params(2),
        name="modulation",
    )(cc, mod_w, mod_b.reshape(depth, 1, d3))


def _lru_rate(lam):
    neg = -lam
    return LRU_C * (jnp.maximum(neg, 0.0) + jnp.log1p(jnp.exp(-jnp.abs(neg))))


def _lru_coeffs(xe_s, r, nb, cw_ref, cb_ref, wg_ref, br_ref, bi_ref, lam_ref, a_s, b_s,
                between_heads=None):
    w = a_s.shape[-1]
    rate = _lru_rate(lam_ref[...])
    hd = w // N_LRU_HEADS
    for h in range(N_LRU_HEADS):
        if between_heads is not None and h > 0:
            between_heads(h - 1)
        sl = slice(h * hd, (h + 1) * hd)
        u = cb_ref[:, sl] + cw_ref[0:1, sl] * xe_s[0:r, sl]
        for k in range(1, SHORT_CONV_WIDTH):
            u = u + cw_ref[k:k + 1, sl] * xe_s[k * nb:k * nb + r, sl]
        g = jnp.dot(u.astype(BF16), wg_ref[h], preferred_element_type=F32)
        rg = _sigmoid(g[:, :hd] + br_ref[:, sl])
        ig = _sigmoid(g[:, hd:] + bi_ref[:, sl])
        p = rg * rate[:, sl]
        a = jnp.exp(-p)
        a_s[:, sl] = a
        y = jnp.tanh(p) * (a * a + 1.0)
        b_s[:, sl] = jnp.exp2(0.5 * jnp.log2(y)) * (ig * u)
    if between_heads is not None:
        between_heads(N_LRU_HEADS - 1)


def _lru_steps(a_s, b_s, h, hs_ref, steps, nb, reverse):
    def step(s, hcur):
        t = steps - 1 - s if reverse else s
        rows = pl.ds(pl.multiple_of(t * nb, nb), nb)
        hnew = a_s[rows, :] * hcur + b_s[rows, :]
        hs_ref[rows, :] = hnew
        return hnew

    return lax.fori_loop(0, steps, step, h)


def _lru_kernel(prev_ref, cur_ref, next_ref, cw_ref, cb_ref, wg_ref, br_ref, bi_ref,
                lam_ref, h0_ref, hs_ref, ht_ref, xe_s, a_s, b_s, h_s, *, nb, reverse):
    i = pl.program_id(0)
    nc = pl.num_programs(0)
    c = nc - 1 - i if reverse else i
    r, w = cur_ref.shape
    lead = SHORT_CONV_LEAD * nb

    @pl.when(i == 0)
    def _():
        h_s[...] = h0_ref[...]

    @pl.when(c > 0)
    def _():
        xe_s[0:lead] = prev_ref[...].astype(F32)

    @pl.when(c == 0)
    def _():
        xe_s[0:lead] = jnp.zeros((lead, w), F32)

    xe_s[lead:lead + r] = cur_ref[...].astype(F32)

    @pl.when(c < nc - 1)
    def _():
        xe_s[lead + r:lead + r + nb] = next_ref[...].astype(F32)

    @pl.when(c == nc - 1)
    def _():
        xe_s[lead + r:lead + r + nb] = jnp.zeros((nb, w), F32)

    _lru_coeffs(xe_s, r, nb, cw_ref, cb_ref, wg_ref, br_ref, bi_ref, lam_ref, a_s, b_s)
    hfin = _lru_steps(a_s, b_s, h_s[...], hs_ref, r // nb, nb, reverse)
    h_s[...] = hfin
    ht_ref[...] = hfin


def _lru_scan(xb, sconv_w, sconv_b, wg, b_r, b_i, lam, h0, *, steps, reverse, name):
    n, w = xb.shape
    nb = h0.shape[0]
    r = steps * nb
    nc = n // r
    lead = SHORT_CONV_LEAD * nb
    chunk = (lambda i: nc - 1 - i) if reverse else (lambda i: i)
    const = lambda i: (0, 0)
    kern = functools.partial(_lru_kernel, nb=nb, reverse=reverse)
    return pl.pallas_call(
        kern,
        grid=(nc,),
        in_specs=[
            pl.BlockSpec((lead, w), lambda i: (jnp.maximum(chunk(i) * (r // lead) - 1, 0), 0)),
            pl.BlockSpec((r, w), lambda i: (chunk(i), 0)),
            pl.BlockSpec((nb, w), lambda i: (jnp.minimum((chunk(i) + 1) * (r // nb), n // nb - 1), 0)),
            pl.BlockSpec((SHORT_CONV_WIDTH, w), const),
            pl.BlockSpec((1, w), const),
            pl.BlockSpec(wg.shape, lambda i: (0, 0, 0)),
            pl.BlockSpec((1, w), const),
            pl.BlockSpec((1, w), const),
            pl.BlockSpec((1, w), const),
            pl.BlockSpec((nb, w), const),
        ],
        out_specs=[pl.BlockSpec((r, w), lambda i: (chunk(i), 0)),
                   pl.BlockSpec((nb, w), const)],
        out_shape=[jax.ShapeDtypeStruct((n, w), F32),
                   jax.ShapeDtypeStruct((nb, w), F32)],
        scratch_shapes=[pltpu.VMEM((lead + r + nb, w), F32),
                        pltpu.VMEM((r, w), F32),
                        pltpu.VMEM((r, w), F32),
                        pltpu.VMEM((nb, w), F32)],
        compiler_params=_params(1),
        name=name,
    )(xb, xb, xb, sconv_w, sconv_b.reshape(1, w), wg, b_r.reshape(1, w), b_i.reshape(1, w),
      lam.reshape(1, w), h0)


def _time_major(x_ref):
    nb, steps, d = x_ref.shape
    return jnp.swapaxes(x_ref[...], 0, 1).reshape(steps * nb, d)


def _ctx_inproj_kernel(x_ref, g_ref, shift_ref, scale_ref, w_ref, o_ref, *, nb):
    hx = _modulated_rmsnorm(_time_major(x_ref), g_ref[...], shift_ref[...], scale_ref[...],
                            nb).astype(BF16)
    o_ref[...] = jnp.dot(hx, w_ref[...], preferred_element_type=F32).astype(o_ref.dtype)


def _ctx_inproj(x, norm_g, shift, scale, w_bf16, steps):
    nb, seq, d = x.shape
    n = seq * nb
    tm = steps * nb
    width = w_bf16.shape[1]
    const = lambda i: (0, 0)
    return pl.pallas_call(
        functools.partial(_ctx_inproj_kernel, nb=nb),
        grid=(n // tm,),
        in_specs=[pl.BlockSpec((nb, steps, d), lambda i: (0, i, 0)),
                  pl.BlockSpec((1, d), const),
                  pl.BlockSpec((nb, d), const),
                  pl.BlockSpec((nb, d), const),
                  _resident(w_bf16.shape, const)],
        out_specs=pl.BlockSpec((tm, width), lambda i: (i, 0)),
        out_shape=jax.ShapeDtypeStruct((n, width), BF16),
        compiler_params=_params(1),
        name="ctx_inproj",
    )(x, norm_g.reshape(1, d), shift, scale, w_bf16)


def _even_in_kernel(x_ref, g_ref, shift_ref, scale_ref, w_ref, cw_ref, cb_ref, wg_ref, br_ref,
                    bi_ref, lam_ref, h0_ref, rows_ref, u_ref, sga_ref, xb_ref, sgb_ref, hs_ref,
                    ht_ref, xe_s, a_s, b_s, h_s, *, nb):
    c = pl.program_id(0)
    nc = pl.num_programs(0) - 1
    r, w = u_ref.shape
    lead = SHORT_CONV_LEAD * nb

    @pl.when(c == 0)
    def _():
        h_s[...] = h0_ref[...]
        xe_s[...] = jnp.zeros(xe_s.shape, F32)

    xe_s[0:lead + r] = xe_s[r:lead + 2 * r]

    x = _time_major(x_ref)
    rows_ref[...] = x
    hx = _modulated_rmsnorm(x, g_ref[...], shift_ref[...], scale_ref[...], nb).astype(BF16)

    def proj(group, cols=slice(0, w)):
        return jnp.dot(hx, w_ref[:, group * w + cols.start:group * w + cols.stop],
                       preferred_element_type=F32)

    xb = proj(3)
    xb_ref[...] = xb.astype(xb_ref.dtype)
    xe_s[lead + r:lead + 2 * r] = xb * (c < nc).astype(F32)

    q = w // (N_LRU_HEADS // 2)

    def proj_slice(i):
        if i < N_LRU_HEADS // 2:
            cols = slice(i * q, (i + 1) * q)
            u_ref[:, cols] = (proj(0, cols) * _sigmoid(proj(1, cols))).astype(u_ref.dtype)
        else:
            j = i - N_LRU_HEADS // 2
            o_ref, group = (sga_ref, 2) if j < N_LRU_HEADS // 4 else (sgb_ref, 4)
            cols = slice((j % (N_LRU_HEADS // 4)) * 2 * q, (j % (N_LRU_HEADS // 4) + 1) * 2 * q)
            o_ref[:, cols] = _silu(proj(group, cols)).astype(o_ref.dtype)

    _lru_coeffs(xe_s, r, nb, cw_ref, cb_ref, wg_ref, br_ref, bi_ref, lam_ref, a_s, b_s,
                between_heads=proj_slice)

    h_prev = h_s[...]
    hfin = _lru_steps(a_s, b_s, h_prev, hs_ref, r // nb, nb, reverse=False)
    hfin = jnp.where(c > 0, hfin, h_prev)
    h_s[...] = hfin
    ht_ref[...] = hfin


def _even_in(x, norm_g, shift, scale, w_in_bf16, sconv_w, sconv_b, wg, b_r, b_i, lam, h0,
             *, steps):
    nb, seq, d = x.shape
    n = seq * nb
    w = sconv_w.shape[-1]
    r = steps * nb
    nc = n // r
    lead = SHORT_CONV_LEAD * nb
    assert w_in_bf16.shape == (d, 5 * w)
    const = lambda i: (0, 0)
    cur = lambda i: (jnp.minimum(i, nc - 1), 0)
    delayed = lambda i: (jnp.maximum(i - 1, 0), 0)
    kern = functools.partial(_even_in_kernel, nb=nb)
    act = jax.ShapeDtypeStruct((n, w), BF16)
    return pl.pallas_call(
        kern,
        grid=(nc + 1,),
        in_specs=[pl.BlockSpec((nb, steps, d), lambda i: (0, jnp.minimum(i, nc - 1), 0)),
                  pl.BlockSpec((1, d), const),
                  pl.BlockSpec((nb, d), const),
                  pl.BlockSpec((nb, d), const),
                  _resident(w_in_bf16.shape, const),
                  pl.BlockSpec((SHORT_CONV_WIDTH, w), const),
                  pl.BlockSpec((1, w), const),
                  _resident(wg.shape, lambda i: (0, 0, 0)),
                  pl.BlockSpec((1, w), const),
                  pl.BlockSpec((1, w), const),
                  pl.BlockSpec((1, w), const),
                  pl.BlockSpec((nb, w), const)],
        out_specs=[pl.BlockSpec((r, d), cur),
                   pl.BlockSpec((r, w), cur),
                   pl.BlockSpec((r, w), cur),
                   pl.BlockSpec((r, w), cur),
                   pl.BlockSpec((r, w), cur),
                   pl.BlockSpec((r, w), delayed),
                   pl.BlockSpec((nb, w), const)],
        out_shape=[jax.ShapeDtypeStruct((n, d), F32), act, act, act, act,
                   jax.ShapeDtypeStruct((n, w), F32),
                   jax.ShapeDtypeStruct((nb, w), F32)],
        scratch_shapes=[pltpu.VMEM((lead + 2 * r, w), F32),
                        pltpu.VMEM((r, w), F32),
                        pltpu.VMEM((r, w), F32),
                        pltpu.VMEM((nb, w), F32)],
        compiler_params=_params(1),
        name="even_in",
    )(x, norm_g.reshape(1, d), shift, scale, w_in_bf16, sconv_w, sconv_b.reshape(1, w), wg,
      b_r.reshape(1, w), b_i.reshape(1, w), lam.reshape(1, w), h0)


def _even_out_kernel(uprev_ref, ucur_ref, unext_ref, sga_ref, sgb_ref, hf_ref, x_ref, gate_ref,
                     cw_ref, cb_ref, lg_ref, lb_ref, wo_ref, xprev_ref, xcur_ref, xnext_ref,
                     scw_ref, scb_ref, wg_ref, br_ref, bi_ref, lam_ref, h0_ref, o_ref,
                     ue_s, v_s, wrep_s, xe_s, a_s, b_s, h_s, out_s, *, nb, row_tile):
    i = pl.program_id(0)
    nc = pl.num_programs(0)
    c = nc - 1 - i
    r, w = ucur_ref.shape
    d = o_ref.shape[-1]
    halo = CONV_HALF * nb
    lead = SHORT_CONV_LEAD * nb
    n_lane_tiles = w // LANES

    @pl.when(i == 0)
    def _():
        h_s[...] = h0_ref[...]
        for k in range(CONV_WIDTH):
            wrep_s[k] = jnp.broadcast_to(cw_ref[k:k + 1, :], (SUBLANES, w))

    @pl.when(c > 0)
    def _():
        xe_s[0:lead] = xprev_ref[...].astype(F32)

    @pl.when(c == 0)
    def _():
        xe_s[0:lead] = jnp.zeros((lead, w), F32)

    xe_s[lead:lead + r] = xcur_ref[...].astype(F32)

    @pl.when(c < nc - 1)
    def _():
        xe_s[lead + r:lead + r + nb] = xnext_ref[...].astype(F32)

    @pl.when(c == nc - 1)
    def _():
        xe_s[lead + r:lead + r + nb] = jnp.zeros((nb, w), F32)

    def copy_in(dst_rows, src_ref):
        for lt in range(n_lane_tiles):
            ue_s[lt, dst_rows] = src_ref[:, lt * LANES:(lt + 1) * LANES].astype(F32)

    def fill(dst_rows, src_ref, valid):
        pl.when(valid)(lambda: copy_in(dst_rows, src_ref))

        @pl.when(jnp.logical_not(valid))
        def _():
            for lt in range(n_lane_tiles):
                ue_s[lt, dst_rows] = jnp.zeros((r, LANES), F32)

    fill(slice(0, r), uprev_ref, c > 0)
    copy_in(slice(r, 2 * r), ucur_ref)
    fill(slice(2 * r, 3 * r), unext_ref, c < nc - 1)

    n_row_tiles = r // row_tile
    sub = row_tile // SUBLANES
    for lt in range(n_lane_tiles):
        ls = slice(lt * LANES, (lt + 1) * LANES)
        bias = jnp.broadcast_to(cb_ref[:, ls], (SUBLANES, LANES))[None]

        def body(j, carry, lt=lt, ls=ls, bias=bias):
            base = pl.multiple_of(j * row_tile, row_tile) + (r - halo)
            def term(k):
                tap = ue_s[lt, pl.ds(base + k * nb, row_tile), :].reshape(sub, SUBLANES, LANES)
                return wrep_s[k, :, ls][None] * tap

            acc = jnp.broadcast_to(bias, (sub, SUBLANES, LANES))
            for k in range(CONV_WIDTH):
                acc = acc + term(k)
            v_s[lt, pl.ds(pl.multiple_of(j * row_tile, row_tile), row_tile), :] = (
                acc.reshape(row_tile, LANES))
            return carry

        lax.fori_loop(0, n_row_tiles, body, 0)

    v = jnp.concatenate([v_s[lt] for lt in range(n_lane_tiles)], axis=-1)
    mu = jnp.mean(v, axis=-1, keepdims=True)
    vc = v - mu
    var = jnp.mean(vc * vc, axis=-1, keepdims=True)
    yn = vc * lax.rsqrt(var + LN_EPS) * lg_ref[...] + lb_ref[...]
    ya = (_silu(yn) * sga_ref[...].astype(F32)).astype(BF16)

    q = d // (N_LRU_HEADS // 2)

    def out_slice(h):
        if h % 2 == 1:
            cols = slice((h // 2) * q, (h // 2 + 1) * q)
            out_s[:, cols] = jnp.dot(ya, wo_ref[0:w, cols], preferred_element_type=F32)

    _lru_coeffs(xe_s, r, nb, scw_ref, scb_ref, wg_ref, br_ref, bi_ref, lam_ref, a_s, b_s,
                between_heads=out_slice)
    h_s[...] = _lru_steps(a_s, b_s, h_s[...], b_s, r // nb, nb, reverse=True)

    yb = ((hf_ref[...] + b_s[...]) * sgb_ref[...].astype(F32)).astype(BF16)
    out = out_s[...] + jnp.dot(yb, wo_ref[w:2 * w, :], preferred_element_type=F32)
    x = x_ref[...].reshape(r // nb, nb, d)
    o_ref[...] = (x + gate_ref[...][None] * out.reshape(r // nb, nb, d)).reshape(r, d)


def _even_out(u, sga, sgb, hf, xb, rows, gate, conv_w, conv_b, ln_g, ln_b, w_out_bf16,
              sconv_w, sconv_b, wg, b_r, b_i, lam, h0, *, steps):
    n, w = u.shape
    d = rows.shape[1]
    nb = gate.shape[0]
    r = steps * nb
    nc = n // r
    lead = SHORT_CONV_LEAD * nb
    assert steps >= CONV_HALF
    const = lambda i: (0, 0)
    chunk = lambda i: nc - 1 - i
    tile = lambda i: (chunk(i), 0)
    kern = functools.partial(_even_out_kernel, nb=nb, row_tile=128)
    return pl.pallas_call(
        kern,
        grid=(nc,),
        in_specs=[
            pl.BlockSpec((r, w), lambda i: (jnp.maximum(chunk(i) - 1, 0), 0)),
            pl.BlockSpec((r, w), tile),
            pl.BlockSpec((r, w), lambda i: (jnp.minimum(chunk(i) + 1, nc - 1), 0)),
            pl.BlockSpec((r, w), tile),
            pl.BlockSpec((r, w), tile),
            pl.BlockSpec((r, w), tile),
            pl.BlockSpec((r, d), tile),
            pl.BlockSpec((nb, d), const),
            pl.BlockSpec((CONV_WIDTH, w), const),
            pl.BlockSpec((1, w), const),
            pl.BlockSpec((1, w), const),
            pl.BlockSpec((1, w), const),
            _resident(w_out_bf16.shape, const),
            pl.BlockSpec((lead, w), lambda i: (jnp.maximum(chunk(i) * (r // lead) - 1, 0), 0)),
            pl.BlockSpec((r, w), tile),
            pl.BlockSpec((nb, w), lambda i: (jnp.minimum((chunk(i) + 1) * (r // nb), n // nb - 1), 0)),
            pl.BlockSpec((SHORT_CONV_WIDTH, w), const),
            pl.BlockSpec((1, w), const),
            _resident(wg.shape, lambda i: (0, 0, 0)),
            pl.BlockSpec((1, w), const),
            pl.BlockSpec((1, w), const),
            pl.BlockSpec((1, w), const),
            pl.BlockSpec((nb, w), const),
        ],
        out_specs=pl.BlockSpec((r, d), tile),
        out_shape=jax.ShapeDtypeStruct((n, d), F32),
        scratch_shapes=[pltpu.VMEM((w // LANES, 3 * r, LANES), F32),
                        pltpu.VMEM((w // LANES, r, LANES), F32),
                        pltpu.VMEM((CONV_WIDTH, SUBLANES, w), F32),
                        pltpu.VMEM((lead + r + nb, w), F32),
                        pltpu.VMEM((r, w), F32),
                        pltpu.VMEM((r, w), F32),
                        pltpu.VMEM((nb, w), F32),
                        pltpu.VMEM((r, d), F32)],
        compiler_params=_params(1),
        name="even_out",
    )(u, u, u, sga, sgb, hf, rows, gate, conv_w, conv_b.reshape(1, w), ln_g.reshape(1, w),
      ln_b.reshape(1, w), w_out_bf16, xb, xb, xb, sconv_w, sconv_b.reshape(1, w), wg,
      b_r.reshape(1, w), b_i.reshape(1, w), lam.reshape(1, w), h0)


def _odd_kernel(x_ref, ng_ref, shift_ref, scale_ref, gate_ref, win_ref, wgrp_ref, pscale_ref,
                wo_ref, fg_ref, o_ref, ue_s, inv_s, *, nbb):
    first = (pl.program_id(0) == 0) & (pl.program_id(1) == 0)
    gw, _, d = x_ref.shape
    n_groups = len(POOL_WINDOWS)
    wp = pscale_ref.shape[-1]
    gd = wp // n_groups
    r = gw * nbb
    pad = (max(POOL_WINDOWS) // 2) * nbb

    @pl.when(first)
    def _():
        t = lax.broadcasted_iota(jnp.int32, (r, LANES), 0) // nbb
        for gi, win in enumerate(POOL_WINDOWS):
            half = win // 2
            cnt = jnp.minimum(t + half, gw) - jnp.maximum(t - half, 0)
            inv_s[gi] = 1.0 / cnt.astype(F32)
        zeros = jnp.zeros((pad, gd), F32)
        for buf in range(ue_s.shape[0]):
            ue_s[buf, 0:pad] = zeros
            ue_s[buf, pad + r:pad + r + pad] = zeros

    x = x_ref[...]
    hx = _modulated_rmsnorm(x.reshape(r, d), ng_ref[...], shift_ref[...], scale_ref[...],
                            nbb).astype(BF16)
    acc = None
    for gi, win in enumerate(POOL_WINDOWS):
        gs = slice(gi * gd, (gi + 1) * gd)
        buf = gi % ue_s.shape[0]
        ue_s[buf, pad:pad + r] = jnp.dot(hx, win_ref[:, gs], preferred_element_type=F32)
        sgate = _silu(jnp.dot(hx, win_ref[:, wp + gi * gd:wp + (gi + 1) * gd],
                              preferred_element_type=F32))
        e = ue_s[buf]
        cur = e[0:r + 2 * pad - nbb] + e[nbb:r + 2 * pad]
        lo = 1
        span = 2
        while span < win:
            sh = (span // 2) * nbb
            cur = cur[0:cur.shape[0] - 2 * sh] + cur[2 * sh:]
            lo += span // 2
            span *= 2
        start = pad - lo * nbb
        inv = jnp.concatenate([inv_s[gi]] * (gd // LANES), axis=-1)
        dlt = (cur[start:start + r] * inv - ue_s[buf, pad:pad + r]).astype(BF16)
        y = jnp.dot(dlt, wgrp_ref[gi], preferred_element_type=F32)
        y = y * pscale_ref[:, gs] * sgate
        part = jnp.dot(y.astype(BF16), wo_ref[gs, :], preferred_element_type=F32)
        acc = part if acc is None else acc + part

    xo = x + gate_ref[...][None] * acc.reshape(gw, nbb, d)
    ms = jnp.mean(xo * xo, axis=-1, keepdims=True)
    o_ref[...] = jnp.swapaxes(xo * lax.rsqrt(ms + RMS_EPS) * fg_ref[...][None], 0, 1)


def _odd_layer(rows3, norm_g, shift, scale, gate, w_in_bf16, w_grp_bf16, pscale, w_out_bf16,
               final_g, *, nbb):
    seq, nb, d = rows3.shape
    n_groups = len(POOL_WINDOWS)
    wp = pscale.shape[-1]
    gd = wp // n_groups
    pad = (max(POOL_WINDOWS) // 2) * nbb
    r = GRID_W * nbb
    blk = lambda i, j: (i, j, 0)
    per_batch = lambda i, j: (j, 0)
    const = lambda i, j: (0, 0)
    kern = functools.partial(_odd_kernel, nbb=nbb)
    return pl.pallas_call(
        kern,
        grid=(seq // GRID_W, nb // nbb),
        in_specs=[
            pl.BlockSpec((GRID_W, nbb, d), blk),
            pl.BlockSpec((1, d), const),
            pl.BlockSpec((nbb, d), per_batch),
            pl.BlockSpec((nbb, d), per_batch),
            pl.BlockSpec((nbb, d), per_batch),
            _resident(w_in_bf16.shape, const),
            _resident(w_grp_bf16.shape, lambda i, j: (0, 0, 0)),
            pl.BlockSpec((1, wp), const),
            _resident(w_out_bf16.shape, const),
            pl.BlockSpec((1, d), const),
        ],
        out_specs=pl.BlockSpec((nbb, GRID_W, d), lambda i, j: (j, i, 0)),
        out_shape=jax.ShapeDtypeStruct((nb, seq, d), F32),
        scratch_shapes=[pltpu.VMEM((2, r + 2 * pad, gd), F32),
                        pltpu.VMEM((n_groups, r, LANES), F32)],
        compiler_params=_params(2),
        name="odd_layer",
    )(rows3, norm_g.reshape(1, d), shift, scale, gate, w_in_bf16, w_grp_bf16,
      pscale.reshape(1, wp), w_out_bf16, final_g.reshape(1, d))


def kernel(x, c, ctx, c_ctx, norm_g, mod_w, mod_b, ev_w_in, ev_conv_w, ev_conv_b, ev_ln_g, ev_ln_b,
           ev_sconv_w, ev_sconv_b, ev_w_r, ev_b_r, ev_w_i, ev_b_i, ev_lam, ev_w_out,
           od_w_in, od_w_grp, od_scale, od_w_out, final_g):
    bn, seq, d = x.shape
    n_ctx = ctx.shape[1]
    w_conv = ev_conv_w.shape[-1]
    w_lru = ev_sconv_w.shape[-1]
    assert w_conv == w_lru

    n_cond = -(-(bn + 1) // SUBLANES) * SUBLANES
    cc = jnp.zeros((n_cond, d), F32).at[:bn].set(c).at[bn].set(c_ctx)
    mod = _modulation(cc, mod_w, mod_b)
    shift0, scale0, gate0 = (mod[0, :bn, k * d:(k + 1) * d] for k in range(3))
    shift1, scale1, gate1 = (mod[1, :bn, k * d:(k + 1) * d] for k in range(3))
    shift_c = jnp.broadcast_to(mod[0, bn:bn + 1, 0:d], (bn, d))
    scale_c = jnp.broadcast_to(mod[0, bn:bn + 1, d:2 * d], (bn, d))

    w_in = ev_w_in[0].astype(BF16)
    xb_col = 3 * w_conv
    xb_ctx = _ctx_inproj(ctx, norm_g[0], shift_c, scale_c, w_in[:, xb_col:xb_col + w_lru], 16)

    def lru_params(k):
        wg = jnp.concatenate([ev_w_r[0, k], ev_w_i[0, k]], axis=-1).astype(BF16)
        return dict(sconv_w=ev_sconv_w[0], sconv_b=ev_sconv_b[0], wg=wg, b_r=ev_b_r[0, k],
                    b_i=ev_b_i[0, k], lam=ev_lam[0, k])

    zero_state = jnp.zeros((bn, w_lru), F32)
    _, h_ctx_f = _lru_scan(xb_ctx, h0=zero_state, steps=16, reverse=False, name="ctx_scan0",
                           **lru_params(0))
    _, h_ctx_b = _lru_scan(xb_ctx, h0=zero_state, steps=16, reverse=True, name="ctx_scan1",
                           **lru_params(1))

    rows, u, sga, xb, sgb, hf, _ = _even_in(x, norm_g[0], shift0, scale0, w_in, h0=h_ctx_f,
                                            steps=16, **lru_params(0))
    rows = _even_out(u, sga, sgb, hf, xb, rows, gate0, ev_conv_w[0], ev_conv_b[0],
                     ev_ln_g[0], ev_ln_b[0], ev_w_out[0].astype(BF16), h0=h_ctx_b, steps=16,
                     **lru_params(1))

    return _odd_layer(rows.reshape(seq, bn, d), norm_g[1], shift1, scale1, gate1,
                      od_w_in[0].astype(BF16), od_w_grp[0].astype(BF16), od_scale[0],
                      od_w_out[0].astype(BF16), final_g, nbb=16)
```

```python
import functools

import jax
import jax.numpy as jnp
from jax import lax
from jax.experimental import pallas as pl
from jax.experimental.pallas import tpu as pltpu

F32 = jnp.float32
BF16 = jnp.bfloat16

RMS_EPS = 1e-6
LN_EPS = 1e-5
LRU_C = 8.0
LOG2E = 1.4426950408889634
N_LRU_HEADS = 8
CONV_WIDTH = 31
CONV_HALF = 15
SHORT_CONV_WIDTH = 4
SHORT_CONV_LEAD = 2
GRID_W = 64
POOL_WINDOWS = (2, 4, 8, 16)

LANES = 128
SUBLANES = 8
VMEM_LIMIT_BYTES = 56 * 1024 * 1024


def _params(n_axes):
    return pltpu.CompilerParams(
        dimension_semantics=("arbitrary",) * n_axes,
        vmem_limit_bytes=VMEM_LIMIT_BYTES)


def _resident(shape, index_map):
    return pl.BlockSpec(shape, index_map, pipeline_mode=pl.Buffered(1))


def _sigmoid(x):
    return 0.5 * jnp.tanh(0.5 * x) + 0.5


def _silu(x):
    return x * _sigmoid(x)


def _modulated_rmsnorm(x, g, shift, scale, nb):
    rows, d = x.shape
    ms = jnp.mean(x * x, axis=-1, keepdims=True)
    y = x * lax.rsqrt(ms + RMS_EPS) * g
    y = y.reshape(rows // nb, nb, d) * (1.0 + scale)[None] + shift[None]
    return y.reshape(rows, d)


def _mod_kernel(c_ref, w_ref, b_ref, o_ref):
    s = _silu(c_ref[...])
    o_ref[0] = jnp.dot(s, w_ref[0], preferred_element_type=F32,
                       precision=lax.Precision.HIGHEST) + b_ref[0]


def _modulation(cc, mod_w, mod_b):
    depth, d, d3 = mod_w.shape
    r = cc.shape[0]
    tn = 1024
    return pl.pallas_call(
        _mod_kernel,
        grid=(depth, d3 // tn),
        in_specs=[pl.BlockSpec((r, d), lambda l, j: (0, 0)),
                  pl.BlockSpec((1, d, tn), lambda l, j: (l, 0, j)),
                  pl.BlockSpec((1, 1, tn), lambda l, j: (l, 0, j))],
        out_specs=pl.BlockSpec((1, r, tn), lambda l, j: (l, 0, j)),
        out_shape=jax.ShapeDtypeStruct((depth, r, d3), F32),
        compiler_params=_params(2),
        name="modulation",
    )(cc, mod_w, mod_b.reshape(depth, 1, d3))


def _lru_rate(lam):
    neg = -lam
    return LRU_C * (jnp.maximum(neg, 0.0) + jnp.log1p(jnp.exp(-jnp.abs(neg))))


def _short_conv(xe_s, r, nb, cw_ref, cb_ref, sl):
    u = cb_ref[:, sl] + cw_ref[0:1, sl] * xe_s[0:r, sl]
    for k in range(1, SHORT_CONV_WIDTH):
        u = u + cw_ref[k:k + 1, sl] * xe_s[k * nb:k * nb + r, sl]
    return u


def _lru_coeffs(conv_out, wg_ref, br_ref, bi_ref, lam_ref, a_s, b_s, between_heads=None):
    w = a_s.shape[-1]
    rate = _lru_rate(lam_ref[...])
    neg_rate_log2e = rate * (-LOG2E)
    half_br = 0.5 * br_ref[...]
    half_bi = 0.5 * bi_ref[...]
    hd = w // N_LRU_HEADS
    for h in range(N_LRU_HEADS):
        if between_heads is not None and h > 0:
            between_heads(h - 1)
        sl = slice(h * hd, (h + 1) * hd)
        u = conv_out(sl)
        g = jnp.dot(u.astype(BF16), wg_ref[h], preferred_element_type=F32)
        rg = 0.5 * jnp.tanh(g[:, :hd] + half_br[:, sl]) + 0.5
        ig = 0.5 * jnp.tanh(g[:, hd:] + half_bi[:, sl]) + 0.5
        p = rg * rate[:, sl]
        a = jnp.exp2(rg * neg_rate_log2e[:, sl])
        a_s[:, sl] = a
        y = jnp.tanh(p) * (a * a + 1.0)
        b_s[:, sl] = jnp.exp2(0.5 * jnp.log2(y)) * (ig * u)
    if between_heads is not None:
        between_heads(N_LRU_HEADS - 1)


def _lru_steps(a_s, b_s, h, hs_ref, steps, nb, reverse):
    def step(s, hcur):
        t = steps - 1 - s if reverse else s
        rows = pl.ds(pl.multiple_of(t * nb, nb), nb)
        hnew = a_s[rows, :] * hcur + b_s[rows, :]
        hs_ref[rows, :] = hnew
        return hnew

    return lax.fori_loop(0, steps, step, h)


def _lru_kernel(prev_ref, cur_ref, next_ref, cw_ref, cb_ref, wg_ref, br_ref, bi_ref,
                lam_ref, h0_ref, hs_ref, ht_ref, xe_s, a_s, b_s, h_s, *, nb, reverse):
    i = pl.program_id(0)
    nc = pl.num_programs(0)
    c = nc - 1 - i if reverse else i
    r, w = cur_ref.shape
    lead = SHORT_CONV_LEAD * nb

    @pl.when(i == 0)
    def _():
        h_s[...] = h0_ref[...]

    @pl.when(c > 0)
    def _():
        xe_s[0:lead] = prev_ref[...].astype(F32)

    @pl.when(c == 0)
    def _():
        xe_s[0:lead] = jnp.zeros((lead, w), F32)

    xe_s[lead:lead + r] = cur_ref[...].astype(F32)

    @pl.when(c < nc - 1)
    def _():
        xe_s[lead + r:lead + r + nb] = next_ref[...].astype(F32)

    @pl.when(c == nc - 1)
    def _():
        xe_s[lead + r:lead + r + nb] = jnp.zeros((nb, w), F32)

    _lru_coeffs(lambda sl: _short_conv(xe_s, r, nb, cw_ref, cb_ref, sl),
                wg_ref, br_ref, bi_ref, lam_ref, a_s, b_s)
    hfin = _lru_steps(a_s, b_s, h_s[...], hs_ref, r // nb, nb, reverse)
    h_s[...] = hfin
    ht_ref[...] = hfin


def _lru_scan(xb, sconv_w, sconv_b, wg, b_r, b_i, lam, h0, *, steps, reverse, name):
    n, w = xb.shape
    nb = h0.shape[0]
    r = steps * nb
    nc = n // r
    lead = SHORT_CONV_LEAD * nb
    chunk = (lambda i: nc - 1 - i) if reverse else (lambda i: i)
    const = lambda i: (0, 0)
    kern = functools.partial(_lru_kernel, nb=nb, reverse=reverse)
    return pl.pallas_call(
        kern,
        grid=(nc,),
        in_specs=[
            pl.BlockSpec((lead, w), lambda i: (jnp.maximum(chunk(i) * (r // lead) - 1, 0), 0)),
            pl.BlockSpec((r, w), lambda i: (chunk(i), 0)),
            pl.BlockSpec((nb, w), lambda i: (jnp.minimum((chunk(i) + 1) * (r // nb), n // nb - 1), 0)),
            pl.BlockSpec((SHORT_CONV_WIDTH, w), const),
            pl.BlockSpec((1, w), const),
            pl.BlockSpec(wg.shape, lambda i: (0, 0, 0)),
            pl.BlockSpec((1, w), const),
            pl.BlockSpec((1, w), const),
            pl.BlockSpec((1, w), const),
            pl.BlockSpec((nb, w), const),
        ],
        out_specs=[pl.BlockSpec((r, w), lambda i: (chunk(i), 0)),
                   pl.BlockSpec((nb, w), const)],
        out_shape=[jax.ShapeDtypeStruct((n, w), F32),
                   jax.ShapeDtypeStruct((nb, w), F32)],
        scratch_shapes=[pltpu.VMEM((lead + r + nb, w), F32),
                        pltpu.VMEM((r, w), F32),
                        pltpu.VMEM((r, w), F32),
                        pltpu.VMEM((nb, w), F32)],
        compiler_params=_params(1),
        name=name,
    )(xb, xb, xb, sconv_w, sconv_b.reshape(1, w), wg, b_r.reshape(1, w), b_i.reshape(1, w),
      lam.reshape(1, w), h0)


def _time_major(x_ref):
    nb, steps, d = x_ref.shape
    return jnp.swapaxes(x_ref[...], 0, 1).reshape(steps * nb, d)


def _ctx_inproj_kernel(x_ref, g_ref, shift_ref, scale_ref, w_ref, o_ref, *, nb):
    hx = _modulated_rmsnorm(_time_major(x_ref), g_ref[...], shift_ref[...], scale_ref[...],
                            nb).astype(BF16)
    o_ref[...] = jnp.dot(hx, w_ref[...], preferred_element_type=F32).astype(o_ref.dtype)


def _ctx_inproj(x, norm_g, shift, scale, w_bf16, steps):
    nb, seq, d = x.shape
    n = seq * nb
    tm = steps * nb
    width = w_bf16.shape[1]
    const = lambda i: (0, 0)
    return pl.pallas_call(
        functools.partial(_ctx_inproj_kernel, nb=nb),
        grid=(n // tm,),
        in_specs=[pl.BlockSpec((nb, steps, d), lambda i: (0, i, 0)),
                  pl.BlockSpec((1, d), const),
                  pl.BlockSpec((nb, d), const),
                  pl.BlockSpec((nb, d), const),
                  _resident(w_bf16.shape, const)],
        out_specs=pl.BlockSpec((tm, width), lambda i: (i, 0)),
        out_shape=jax.ShapeDtypeStruct((n, width), BF16),
        compiler_params=_params(1),
        name="ctx_inproj",
    )(x, norm_g.reshape(1, d), shift, scale, w_bf16)


def _even_in_kernel(x_ref, g_ref, shift_ref, scale_ref, w_ref, cw_ref, cb_ref, wg_ref, br_ref,
                    bi_ref, lam_ref, h0_ref, rows_ref, u_ref, sga_ref, sgb_ref, sc_ref, hs_ref,
                    ht_ref, xe_s, a_s, b_s, h_s, *, nb):
    c = pl.program_id(0)
    nc = pl.num_programs(0) - 1
    r, w = u_ref.shape
    lead = SHORT_CONV_LEAD * nb

    @pl.when(c == 0)
    def _():
        h_s[...] = h0_ref[...]
        xe_s[...] = jnp.zeros(xe_s.shape, F32)

    xe_s[0:lead + r] = xe_s[r:lead + 2 * r]

    x = _time_major(x_ref)
    rows_ref[...] = x
    hx = _modulated_rmsnorm(x, g_ref[...], shift_ref[...], scale_ref[...], nb).astype(BF16)

    def proj(group, cols=slice(0, w)):
        return jnp.dot(hx, w_ref[:, group * w + cols.start:group * w + cols.stop],
                       preferred_element_type=F32)

    xe_s[lead + r:lead + 2 * r] = proj(3) * (c < nc).astype(F32)

    q = w // (N_LRU_HEADS // 2)

    def proj_slice(i):
        if i < N_LRU_HEADS // 2:
            cols = slice(i * q, (i + 1) * q)
            u_ref[:, cols] = (proj(0, cols) * _sigmoid(proj(1, cols))).astype(u_ref.dtype)
        else:
            j = i - N_LRU_HEADS // 2
            o_ref, group = (sga_ref, 2) if j < N_LRU_HEADS // 4 else (sgb_ref, 4)
            cols = slice((j % (N_LRU_HEADS // 4)) * 2 * q, (j % (N_LRU_HEADS // 4) + 1) * 2 * q)
            o_ref[:, cols] = _silu(proj(group, cols)).astype(o_ref.dtype)

    def conv_out(sl):
        uc = _short_conv(xe_s, r, nb, cw_ref, cb_ref, sl)
        sc_ref[:, sl] = uc.astype(sc_ref.dtype)
        return uc

    _lru_coeffs(conv_out, wg_ref, br_ref, bi_ref, lam_ref, a_s, b_s, between_heads=proj_slice)

    h_prev = h_s[...]
    hfin = _lru_steps(a_s, b_s, h_prev, hs_ref, r // nb, nb, reverse=False)
    hfin = jnp.where(c > 0, hfin, h_prev)
    h_s[...] = hfin
    ht_ref[...] = hfin


def _even_in(x, norm_g, shift, scale, w_in_bf16, sconv_w, sconv_b, wg, b_r, b_i, lam, h0,
             *, steps):
    nb, seq, d = x.shape
    n = seq * nb
    w = sconv_w.shape[-1]
    r = steps * nb
    nc = n // r
    lead = SHORT_CONV_LEAD * nb
    assert w_in_bf16.shape == (d, 5 * w)
    const = lambda i: (0, 0)
    cur = lambda i: (jnp.minimum(i, nc - 1), 0)
    delayed = lambda i: (jnp.maximum(i - 1, 0), 0)
    kern = functools.partial(_even_in_kernel, nb=nb)
    act = jax.ShapeDtypeStruct((n, w), BF16)
    return pl.pallas_call(
        kern,
        grid=(nc + 1,),
        in_specs=[pl.BlockSpec((nb, steps, d), lambda i: (0, jnp.minimum(i, nc - 1), 0)),
                  pl.BlockSpec((1, d), const),
                  pl.BlockSpec((nb, d), const),
                  pl.BlockSpec((nb, d), const),
                  _resident(w_in_bf16.shape, const),
                  pl.BlockSpec((SHORT_CONV_WIDTH, w), const),
                  pl.BlockSpec((1, w), const),
                  _resident(wg.shape, lambda i: (0, 0, 0)),
                  pl.BlockSpec((1, w), const),
                  pl.BlockSpec((1, w), const),
                  pl.BlockSpec((1, w), const),
                  pl.BlockSpec((nb, w), const)],
        out_specs=[pl.BlockSpec((r, d), cur),
                   pl.BlockSpec((r, w), cur),
                   pl.BlockSpec((r, w), cur),
                   pl.BlockSpec((r, w), cur),
                   pl.BlockSpec((r, w), delayed),
                   pl.BlockSpec((r, w), delayed),
                   pl.BlockSpec((nb, w), const)],
        out_shape=[jax.ShapeDtypeStruct((n, d), F32), act, act, act, act,
                   jax.ShapeDtypeStruct((n, w), F32),
                   jax.ShapeDtypeStruct((nb, w), F32)],
        scratch_shapes=[pltpu.VMEM((lead + 2 * r, w), F32),
                        pltpu.VMEM((r, w), F32),
                        pltpu.VMEM((r, w), F32),
                        pltpu.VMEM((nb, w), F32)],
        compiler_params=_params(1),
        name="even_in",
    )(x, norm_g.reshape(1, d), shift, scale, w_in_bf16, sconv_w, sconv_b.reshape(1, w), wg,
      b_r.reshape(1, w), b_i.reshape(1, w), lam.reshape(1, w), h0)


def _even_out_kernel(uprev_ref, ucur_ref, unext_ref, sga_ref, sgb_ref, hf_ref, x_ref, gate_ref,
                     cw_ref, cb_ref, lg_ref, lb_ref, wo_ref, sc_ref, wg_ref, br_ref, bi_ref,
                     lam_ref, h0_ref, o_ref, ue_s, v_s, wrep_s, a_s, b_s, h_s, out_s,
                     *, nb, row_tile):
    i = pl.program_id(0)
    nc = pl.num_programs(0)
    c = nc - 1 - i
    r, w = ucur_ref.shape
    d = o_ref.shape[-1]
    halo = CONV_HALF * nb
    n_lane_tiles = w // LANES

    @pl.when(i == 0)
    def _():
        h_s[...] = h0_ref[...]
        for k in range(CONV_WIDTH):
            wrep_s[k] = jnp.broadcast_to(cw_ref[k:k + 1, :], (SUBLANES, w))

    def copy_in(dst_rows, src_ref):
        for lt in range(n_lane_tiles):
            ue_s[lt, dst_rows] = src_ref[:, lt * LANES:(lt + 1) * LANES].astype(F32)

    def fill(dst_rows, src_ref, valid):
        pl.when(valid)(lambda: copy_in(dst_rows, src_ref))

        @pl.when(jnp.logical_not(valid))
        def _():
            for lt in range(n_lane_tiles):
                ue_s[lt, dst_rows] = jnp.zeros((r, LANES), F32)

    fill(slice(0, r), uprev_ref, c > 0)
    copy_in(slice(r, 2 * r), ucur_ref)
    fill(slice(2 * r, 3 * r), unext_ref, c < nc - 1)

    n_row_tiles = r // row_tile
    sub = row_tile // SUBLANES
    for lt in range(n_lane_tiles):
        ls = slice(lt * LANES, (lt + 1) * LANES)
        bias = jnp.broadcast_to(cb_ref[:, ls], (SUBLANES, LANES))[None]

        def body(j, carry, lt=lt, ls=ls, bias=bias):
            base = pl.multiple_of(j * row_tile, row_tile) + (r - halo)
            def term(k):
                tap = ue_s[lt, pl.ds(base + k * nb, row_tile), :].reshape(sub, SUBLANES, LANES)
                return wrep_s[k, :, ls][None] * tap

            acc = jnp.broadcast_to(bias, (sub, SUBLANES, LANES))
            for k in range(CONV_WIDTH):
                acc = acc + term(k)
            v_s[lt, pl.ds(pl.multiple_of(j * row_tile, row_tile), row_tile), :] = (
                acc.reshape(row_tile, LANES))
            return carry

        lax.fori_loop(0, n_row_tiles, body, 0)

    v = jnp.concatenate([v_s[lt] for lt in range(n_lane_tiles)], axis=-1)
    mu = jnp.mean(v, axis=-1, keepdims=True)
    vc = v - mu
    var = jnp.mean(vc * vc, axis=-1, keepdims=True)
    yn = vc * lax.rsqrt(var + LN_EPS) * lg_ref[...] + lb_ref[...]
    ya = (_silu(yn) * sga_ref[...].astype(F32)).astype(BF16)

    q = d // (N_LRU_HEADS // 2)

    def out_slice(h):
        if h % 2 == 1:
            cols = slice((h // 2) * q, (h // 2 + 1) * q)
            out_s[:, cols] = jnp.dot(ya, wo_ref[0:w, cols], preferred_element_type=F32)

    _lru_coeffs(lambda sl: sc_ref[:, sl].astype(F32), wg_ref, br_ref, bi_ref, lam_ref, a_s, b_s,
                between_heads=out_slice)
    h_s[...] = _lru_steps(a_s, b_s, h_s[...], b_s, r // nb, nb, reverse=True)

    yb = ((hf_ref[...] + b_s[...]) * sgb_ref[...].astype(F32)).astype(BF16)
    out = out_s[...] + jnp.dot(yb, wo_ref[w:2 * w, :], preferred_element_type=F32)
    x = x_ref[...].reshape(r // nb, nb, d)
    o_ref[...] = (x + gate_ref[...][None] * out.reshape(r // nb, nb, d)).reshape(r, d)


def _even_out(u, sga, sgb, hf, sc, rows, gate, conv_w, conv_b, ln_g, ln_b, w_out_bf16,
              wg, b_r, b_i, lam, h0, *, steps):
    n, w = u.shape
    d = rows.shape[1]
    nb = gate.shape[0]
    r = steps * nb
    nc = n // r
    assert steps >= CONV_HALF
    const = lambda i: (0, 0)
    chunk = lambda i: nc - 1 - i
    tile = lambda i: (chunk(i), 0)
    kern = functools.partial(_even_out_kernel, nb=nb, row_tile=128)
    return pl.pallas_call(
        kern,
        grid=(nc,),
        in_specs=[
            pl.BlockSpec((r, w), lambda i: (jnp.maximum(chunk(i) - 1, 0), 0)),
            pl.BlockSpec((r, w), tile),
            pl.BlockSpec((r, w), lambda i: (jnp.minimum(chunk(i) + 1, nc - 1), 0)),
            pl.BlockSpec((r, w), tile),
            pl.BlockSpec((r, w), tile),
            pl.BlockSpec((r, w), tile),
            pl.BlockSpec((r, d), tile),
            pl.BlockSpec((nb, d), const),
            pl.BlockSpec((CONV_WIDTH, w), const),
            pl.BlockSpec((1, w), const),
            pl.BlockSpec((1, w), const),
            pl.BlockSpec((1, w), const),
            _resident(w_out_bf16.shape, const),
            pl.BlockSpec((r, w), tile),
            _resident(wg.shape, lambda i: (0, 0, 0)),
            pl.BlockSpec((1, w), const),
            pl.BlockSpec((1, w), const),
            pl.BlockSpec((1, w), const),
            pl.BlockSpec((nb, w), const),
        ],
        out_specs=pl.BlockSpec((r, d), tile),
        out_shape=jax.ShapeDtypeStruct((n, d), F32),
        scratch_shapes=[pltpu.VMEM((w // LANES, 3 * r, LANES), F32),
                        pltpu.VMEM((w // LANES, r, LANES), F32),
                        pltpu.VMEM((CONV_WIDTH, SUBLANES, w), F32),
                        pltpu.VMEM((r, w), F32),
                        pltpu.VMEM((r, w), F32),
                        pltpu.VMEM((nb, w), F32),
                        pltpu.VMEM((r, d), F32)],
        compiler_params=_params(1),
        name="even_out",
    )(u, u, u, sga, sgb, hf, rows, gate, conv_w, conv_b.reshape(1, w), ln_g.reshape(1, w),
      ln_b.reshape(1, w), w_out_bf16, sc, wg, b_r.reshape(1, w), b_i.reshape(1, w),
      lam.reshape(1, w), h0)


def _odd_kernel(x_ref, ng_ref, shift_ref, scale_ref, gate_ref, win_ref, wgrp_ref, pscale_ref,
                wo_ref, fg_ref, o_ref, ue_s, inv_s, *, nbb):
    first = (pl.program_id(0) == 0) & (pl.program_id(1) == 0)
    gw, _, d = x_ref.shape
    n_groups = len(POOL_WINDOWS)
    wp = pscale_ref.shape[-1]
    gd = wp // n_groups
    r = gw * nbb
    pad = (max(POOL_WINDOWS) // 2) * nbb

    @pl.when(first)
    def _():
        t = lax.broadcasted_iota(jnp.int32, (r, LANES), 0) // nbb
        for gi, win in enumerate(POOL_WINDOWS):
            half = win // 2
            cnt = jnp.minimum(t + half, gw) - jnp.maximum(t - half, 0)
            inv_s[gi] = 1.0 / cnt.astype(F32)
        zeros = jnp.zeros((pad, gd), F32)
        for buf in range(ue_s.shape[0]):
            ue_s[buf, 0:pad] = zeros
            ue_s[buf, pad + r:pad + r + pad] = zeros

    x = x_ref[...]
    hx = _modulated_rmsnorm(x.reshape(r, d), ng_ref[...], shift_ref[...], scale_ref[...],
                            nbb).astype(BF16)
    acc = None
    for gi, win in enumerate(POOL_WINDOWS):
        gs = slice(gi * gd, (gi + 1) * gd)
        buf = gi % ue_s.shape[0]
        ue_s[buf, pad:pad + r] = jnp.dot(hx, win_ref[:, gs], preferred_element_type=F32)
        sgate = _silu(jnp.dot(hx, win_ref[:, wp + gi * gd:wp + (gi + 1) * gd],
                              preferred_element_type=F32))
        e = ue_s[buf]
        cur = e[0:r + 2 * pad - nbb] + e[nbb:r + 2 * pad]
        lo = 1
        span = 2
        while span < win:
            sh = (span // 2) * nbb
            cur = cur[0:cur.shape[0] - 2 * sh] + cur[2 * sh:]
            lo += span // 2
            span *= 2
        start = pad - lo * nbb
        inv = jnp.concatenate([inv_s[gi]] * (gd // LANES), axis=-1)
        dlt = (cur[start:start + r] * inv - ue_s[buf, pad:pad + r]).astype(BF16)
        y = jnp.dot(dlt, wgrp_ref[gi], preferred_element_type=F32)
        y = y * pscale_ref[:, gs] * sgate
        part = jnp.dot(y.astype(BF16), wo_ref[gs, :], preferred_element_type=F32)
        acc = part if acc is None else acc + part

    xo = x + gate_ref[...][None] * acc.reshape(gw, nbb, d)
    ms = jnp.mean(xo * xo, axis=-1, keepdims=True)
    o_ref[...] = jnp.swapaxes(xo * lax.rsqrt(ms + RMS_EPS) * fg_ref[...][None], 0, 1)


def _odd_layer(rows3, norm_g, shift, scale, gate, w_in_bf16, w_grp_bf16, pscale, w_out_bf16,
               final_g, *, nbb):
    seq, nb, d = rows3.shape
    n_groups = len(POOL_WINDOWS)
    wp = pscale.shape[-1]
    gd = wp // n_groups
    pad = (max(POOL_WINDOWS) // 2) * nbb
    r = GRID_W * nbb
    blk = lambda i, j: (i, j, 0)
    per_batch = lambda i, j: (j, 0)
    const = lambda i, j: (0, 0)
    kern = functools.partial(_odd_kernel, nbb=nbb)
    return pl.pallas_call(
        kern,
        grid=(seq // GRID_W, nb // nbb),
        in_specs=[
            pl.BlockSpec((GRID_W, nbb, d), blk),
            pl.BlockSpec((1, d), const),
            pl.BlockSpec((nbb, d), per_batch),
            pl.BlockSpec((nbb, d), per_batch),
            pl.BlockSpec((nbb, d), per_batch),
            _resident(w_in_bf16.shape, const),
            _resident(w_grp_bf16.shape, lambda i, j: (0, 0, 0)),
            pl.BlockSpec((1, wp), const),
            _resident(w_out_bf16.shape, const),
            pl.BlockSpec((1, d), const),
        ],
        out_specs=pl.BlockSpec((nbb, GRID_W, d), lambda i, j: (j, i, 0)),
        out_shape=jax.ShapeDtypeStruct((nb, seq, d), F32),
        scratch_shapes=[pltpu.VMEM((2, r + 2 * pad, gd), F32),
                        pltpu.VMEM((n_groups, r, LANES), F32)],
        compiler_params=_params(2),
        name="odd_layer",
    )(rows3, norm_g.reshape(1, d), shift, scale, gate, w_in_bf16, w_grp_bf16,
      pscale.reshape(1, wp), w_out_bf16, final_g.reshape(1, d))


def kernel(x, c, ctx, c_ctx, norm_g, mod_w, mod_b, ev_w_in, ev_conv_w, ev_conv_b, ev_ln_g, ev_ln_b,
           ev_sconv_w, ev_sconv_b, ev_w_r, ev_b_r, ev_w_i, ev_b_i, ev_lam, ev_w_out,
           od_w_in, od_w_grp, od_scale, od_w_out, final_g):
    bn, seq, d = x.shape
    n_ctx = ctx.shape[1]
    w_conv = ev_conv_w.shape[-1]
    w_lru = ev_sconv_w.shape[-1]
    assert w_conv == w_lru

    n_cond = -(-(bn + 1) // SUBLANES) * SUBLANES
    cc = jnp.zeros((n_cond, d), F32).at[:bn].set(c).at[bn].set(c_ctx)
    mod = _modulation(cc, mod_w, mod_b)
    shift0, scale0, gate0 = (mod[0, :bn, k * d:(k + 1) * d] for k in range(3))
    shift1, scale1, gate1 = (mod[1, :bn, k * d:(k + 1) * d] for k in range(3))
    shift_c = jnp.broadcast_to(mod[0, bn:bn + 1, 0:d], (bn, d))
    scale_c = jnp.broadcast_to(mod[0, bn:bn + 1, d:2 * d], (bn, d))

    w_in = ev_w_in[0].astype(BF16)
    xb_col = 3 * w_conv
    xb_ctx = _ctx_inproj(ctx, norm_g[0], shift_c, scale_c, w_in[:, xb_col:xb_col + w_lru], 16)

    def gate_params(k):
        wg = (0.5 * jnp.concatenate([ev_w_r[0, k], ev_w_i[0, k]], axis=-1)).astype(BF16)
        return dict(wg=wg, b_r=ev_b_r[0, k], b_i=ev_b_i[0, k], lam=ev_lam[0, k])

    sconv = dict(sconv_w=ev_sconv_w[0], sconv_b=ev_sconv_b[0])
    zero_state = jnp.zeros((bn, w_lru), F32)
    _, h_ctx_f = _lru_scan(xb_ctx, h0=zero_state, steps=16, reverse=False, name="ctx_scan0",
                           **sconv, **gate_params(0))
    _, h_ctx_b = _lru_scan(xb_ctx, h0=zero_state, steps=16, reverse=True, name="ctx_scan1",
                           **sconv, **gate_params(1))

    rows, u, sga, sgb, sc, hf, _ = _even_in(x, norm_g[0], shift0, scale0, w_in, h0=h_ctx_f,
                                            steps=16, **sconv, **gate_params(0))
    rows = _even_out(u, sga, sgb, hf, sc, rows, gate0, ev_conv_w[0], ev_conv_b[0],
                     ev_ln_g[0], ev_ln_b[0], ev_w_out[0].astype(BF16), h0=h_ctx_b, steps=16,
                     **gate_params(1))

    return _odd_layer(rows.reshape(seq, bn, d), norm_g[1], shift1, scale1, gate1,
                      od_w_in[0].astype(BF16), od_w_grp[0].astype(BF16), od_scale[0],
                      od_w_out[0].astype(BF16), final_g, nbb=16)
```

```python
import functools

import jax
import jax.numpy as jnp
from jax import lax
from jax.experimental import pallas as pl
from jax.experimental.pallas import tpu as pltpu

F32 = jnp.float32
BF16 = jnp.bfloat16

RMS_EPS = 1e-6
LN_EPS = 1e-5
LRU_C = 8.0
LOG2E = 1.4426950408889634
N_LRU_HEADS = 8
CONV_WIDTH = 31
CONV_HALF = 15
SHORT_CONV_WIDTH = 4
SHORT_CONV_LEAD = 2
TAPS_PER_FENCE = 6
GRID_W = 64
POOL_WINDOWS = (2, 4, 8, 16)

LANES = 128
SUBLANES = 8
VMEM_LIMIT_BYTES = 56 * 1024 * 1024


def _params(n_axes):
    return pltpu.CompilerParams(
        dimension_semantics=("arbitrary",) * n_axes,
        vmem_limit_bytes=VMEM_LIMIT_BYTES)


def _resident(shape, index_map):
    return pl.BlockSpec(shape, index_map, pipeline_mode=pl.Buffered(1))


def _sigmoid(x):
    return 0.5 * jnp.tanh(0.5 * x) + 0.5


def _silu(x):
    return x * _sigmoid(x)


def _modulated_rmsnorm(x, g, shift, scale, nb):
    rows, d = x.shape
    ms = jnp.mean(x * x, axis=-1, keepdims=True)
    y = x * lax.rsqrt(ms + RMS_EPS) * g
    y = y.reshape(rows // nb, nb, d) * (1.0 + scale)[None] + shift[None]
    return y.reshape(rows, d)


def _mod_kernel(c_ref, w_ref, b_ref, o_ref):
    s = _silu(c_ref[...])
    o_ref[0] = jnp.dot(s, w_ref[0], preferred_element_type=F32,
                       precision=lax.Precision.HIGHEST) + b_ref[0]


def _modulation(cc, mod_w, mod_b):
    depth, d, d3 = mod_w.shape
    r = cc.shape[0]
    tn = 1024
    return pl.pallas_call(
        _mod_kernel,
        grid=(depth, d3 // tn),
        in_specs=[pl.BlockSpec((r, d), lambda l, j: (0, 0)),
                  pl.BlockSpec((1, d, tn), lambda l, j: (l, 0, j)),
                  pl.BlockSpec((1, 1, tn), lambda l, j: (l, 0, j))],
        out_specs=pl.BlockSpec((1, r, tn), lambda l, j: (l, 0, j)),
        out_shape=jax.ShapeDtypeStruct((depth, r, d3), F32),
        compiler_params=_params(2),
        name="modulation",
    )(cc, mod_w, mod_b.reshape(depth, 1, d3))


def _lru_rate(lam):
    neg = -lam
    return LRU_C * (jnp.maximum(neg, 0.0) + jnp.log1p(jnp.exp(-jnp.abs(neg))))


def _short_conv(xe_s, r, nb, cw_ref, cb_ref, sl):
    u = cb_ref[:, sl] + cw_ref[0:1, sl] * xe_s[0:r, sl]
    for k in range(1, SHORT_CONV_WIDTH):
        u = u + cw_ref[k:k + 1, sl] * xe_s[k * nb:k * nb + r, sl]
    return u


def _lru_coeffs(conv_out, wg_ref, br_ref, bi_ref, lam_ref, a_s, b_s, between_heads=None):
    w = a_s.shape[-1]
    rate = _lru_rate(lam_ref[...])
    neg_rate_log2e = rate * (-LOG2E)
    half_br = 0.5 * br_ref[...]
    half_bi = 0.5 * bi_ref[...]
    hd = w // N_LRU_HEADS
    for h in range(N_LRU_HEADS):
        if between_heads is not None and h > 0:
            between_heads(h - 1)
        sl = slice(h * hd, (h + 1) * hd)
        u = conv_out(sl)
        g = jnp.dot(u.astype(BF16), wg_ref[h], preferred_element_type=F32)
        rg = 0.5 * jnp.tanh(g[:, :hd] + half_br[:, sl]) + 0.5
        ig = 0.5 * jnp.tanh(g[:, hd:] + half_bi[:, sl]) + 0.5
        p = rg * rate[:, sl]
        a = jnp.exp2(rg * neg_rate_log2e[:, sl])
        a_s[:, sl] = a
        y = jnp.tanh(p) * (a * a + 1.0)
        b_s[:, sl] = jnp.exp2(0.5 * jnp.log2(y)) * (ig * u)
    if between_heads is not None:
        between_heads(N_LRU_HEADS - 1)


def _lru_steps(a_s, b_s, h, hs_ref, steps, nb, reverse):
    def step(s, hcur):
        t = steps - 1 - s if reverse else s
        rows = pl.ds(pl.multiple_of(t * nb, nb), nb)
        hnew = a_s[rows, :] * hcur + b_s[rows, :]
        hs_ref[rows, :] = hnew
        return hnew

    return lax.fori_loop(0, steps, step, h)


def _lru_kernel(prev_ref, cur_ref, next_ref, cw_ref, cb_ref, wg_ref, br_ref, bi_ref,
                lam_ref, h0_ref, hs_ref, ht_ref, xe_s, a_s, b_s, h_s, *, nb, reverse):
    i = pl.program_id(0)
    nc = pl.num_programs(0)
    c = nc - 1 - i if reverse else i
    r, w = cur_ref.shape
    lead = SHORT_CONV_LEAD * nb

    @pl.when(i == 0)
    def _():
        h_s[...] = h0_ref[...]

    @pl.when(c > 0)
    def _():
        xe_s[0:lead] = prev_ref[...].astype(F32)

    @pl.when(c == 0)
    def _():
        xe_s[0:lead] = jnp.zeros((lead, w), F32)

    xe_s[lead:lead + r] = cur_ref[...].astype(F32)

    @pl.when(c < nc - 1)
    def _():
        xe_s[lead + r:lead + r + nb] = next_ref[...].astype(F32)

    @pl.when(c == nc - 1)
    def _():
        xe_s[lead + r:lead + r + nb] = jnp.zeros((nb, w), F32)

    _lru_coeffs(lambda sl: _short_conv(xe_s, r, nb, cw_ref, cb_ref, sl),
                wg_ref, br_ref, bi_ref, lam_ref, a_s, b_s)
    hfin = _lru_steps(a_s, b_s, h_s[...], hs_ref, r // nb, nb, reverse)
    h_s[...] = hfin
    ht_ref[...] = hfin


def _lru_scan(xb, sconv_w, sconv_b, wg, b_r, b_i, lam, h0, *, steps, reverse, name):
    n, w = xb.shape
    nb = h0.shape[0]
    r = steps * nb
    nc = n // r
    lead = SHORT_CONV_LEAD * nb
    chunk = (lambda i: nc - 1 - i) if reverse else (lambda i: i)
    const = lambda i: (0, 0)
    kern = functools.partial(_lru_kernel, nb=nb, reverse=reverse)
    return pl.pallas_call(
        kern,
        grid=(nc,),
        in_specs=[
            pl.BlockSpec((lead, w), lambda i: (jnp.maximum(chunk(i) * (r // lead) - 1, 0), 0)),
            pl.BlockSpec((r, w), lambda i: (chunk(i), 0)),
            pl.BlockSpec((nb, w), lambda i: (jnp.minimum((chunk(i) + 1) * (r // nb), n // nb - 1), 0)),
            pl.BlockSpec((SHORT_CONV_WIDTH, w), const),
            pl.BlockSpec((1, w), const),
            pl.BlockSpec(wg.shape, lambda i: (0, 0, 0)),
            pl.BlockSpec((1, w), const),
            pl.BlockSpec((1, w), const),
            pl.BlockSpec((1, w), const),
            pl.BlockSpec((nb, w), const),
        ],
        out_specs=[pl.BlockSpec((r, w), lambda i: (chunk(i), 0)),
                   pl.BlockSpec((nb, w), const)],
        out_shape=[jax.ShapeDtypeStruct((n, w), F32),
                   jax.ShapeDtypeStruct((nb, w), F32)],
        scratch_shapes=[pltpu.VMEM((lead + r + nb, w), F32),
                        pltpu.VMEM((r, w), F32),
                        pltpu.VMEM((r, w), F32),
                        pltpu.VMEM((nb, w), F32)],
        compiler_params=_params(1),
        name=name,
    )(xb, xb, xb, sconv_w, sconv_b.reshape(1, w), wg, b_r.reshape(1, w), b_i.reshape(1, w),
      lam.reshape(1, w), h0)


def _time_major(x_ref):
    nb, steps, d = x_ref.shape
    return jnp.swapaxes(x_ref[...], 0, 1).reshape(steps * nb, d)


def _ctx_inproj_kernel(x_ref, g_ref, shift_ref, scale_ref, w_ref, o_ref, *, nb):
    hx = _modulated_rmsnorm(_time_major(x_ref), g_ref[...], shift_ref[...], scale_ref[...],
                            nb).astype(BF16)
    o_ref[...] = jnp.dot(hx, w_ref[...], preferred_element_type=F32).astype(o_ref.dtype)


def _ctx_inproj(x, norm_g, shift, scale, w_bf16, steps):
    nb, seq, d = x.shape
    n = seq * nb
    tm = steps * nb
    width = w_bf16.shape[1]
    const = lambda i: (0, 0)
    return pl.pallas_call(
        functools.partial(_ctx_inproj_kernel, nb=nb),
        grid=(n // tm,),
        in_specs=[pl.BlockSpec((nb, steps, d), lambda i: (0, i, 0)),
                  pl.BlockSpec((1, d), const),
                  pl.BlockSpec((nb, d), const),
                  pl.BlockSpec((nb, d), const),
                  _resident(w_bf16.shape, const)],
        out_specs=pl.BlockSpec((tm, width), lambda i: (i, 0)),
        out_shape=jax.ShapeDtypeStruct((n, width), BF16),
        compiler_params=_params(1),
        name="ctx_inproj",
    )(x, norm_g.reshape(1, d), shift, scale, w_bf16)


def _even_in_kernel(x_ref, g_ref, shift_ref, scale_ref, w_ref, cw_ref, cb_ref, wg_ref, br_ref,
                    bi_ref, lam_ref, h0_ref, rows_ref, u_ref, sga_ref, sgb_ref, sc_ref, hs_ref,
                    ht_ref, xe_s, a_s, b_s, h_s, *, nb):
    c = pl.program_id(0)
    nc = pl.num_programs(0) - 1
    r, w = u_ref.shape
    lead = SHORT_CONV_LEAD * nb

    @pl.when(c == 0)
    def _():
        h_s[...] = h0_ref[...]
        xe_s[...] = jnp.zeros(xe_s.shape, F32)

    xe_s[0:lead + r] = xe_s[r:lead + 2 * r]

    x = _time_major(x_ref)
    rows_ref[...] = x
    hx = _modulated_rmsnorm(x, g_ref[...], shift_ref[...], scale_ref[...], nb).astype(BF16)

    def proj(group, cols=slice(0, w)):
        return jnp.dot(hx, w_ref[:, group * w + cols.start:group * w + cols.stop],
                       preferred_element_type=F32)

    xe_s[lead + r:lead + 2 * r] = proj(3) * (c < nc).astype(F32)

    q = w // (N_LRU_HEADS // 2)

    def proj_slice(i):
        if i < N_LRU_HEADS // 2:
            cols = slice(i * q, (i + 1) * q)
            u_ref[:, cols] = (proj(0, cols) * _sigmoid(proj(1, cols))).astype(u_ref.dtype)
        else:
            j = i - N_LRU_HEADS // 2
            o_ref, group = (sga_ref, 2) if j < N_LRU_HEADS // 4 else (sgb_ref, 4)
            cols = slice((j % (N_LRU_HEADS // 4)) * 2 * q, (j % (N_LRU_HEADS // 4) + 1) * 2 * q)
            o_ref[:, cols] = _silu(proj(group, cols)).astype(o_ref.dtype)

    def conv_out(sl):
        uc = _short_conv(xe_s, r, nb, cw_ref, cb_ref, sl)
        sc_ref[:, sl] = uc.astype(sc_ref.dtype)
        return uc

    _lru_coeffs(conv_out, wg_ref, br_ref, bi_ref, lam_ref, a_s, b_s, between_heads=proj_slice)

    h_prev = h_s[...]
    hfin = _lru_steps(a_s, b_s, h_prev, hs_ref, r // nb, nb, reverse=False)
    hfin = jnp.where(c > 0, hfin, h_prev)
    h_s[...] = hfin
    ht_ref[...] = hfin


def _even_in(x, norm_g, shift, scale, w_in_bf16, sconv_w, sconv_b, wg, b_r, b_i, lam, h0,
             *, steps):
    nb, seq, d = x.shape
    n = seq * nb
    w = sconv_w.shape[-1]
    r = steps * nb
    nc = n // r
    lead = SHORT_CONV_LEAD * nb
    assert w_in_bf16.shape == (d, 5 * w)
    const = lambda i: (0, 0)
    cur = lambda i: (jnp.minimum(i, nc - 1), 0)
    delayed = lambda i: (jnp.maximum(i - 1, 0), 0)
    kern = functools.partial(_even_in_kernel, nb=nb)
    act = jax.ShapeDtypeStruct((n, w), BF16)
    return pl.pallas_call(
        kern,
        grid=(nc + 1,),
        in_specs=[pl.BlockSpec((nb, steps, d), lambda i: (0, jnp.minimum(i, nc - 1), 0)),
                  pl.BlockSpec((1, d), const),
                  pl.BlockSpec((nb, d), const),
                  pl.BlockSpec((nb, d), const),
                  _resident(w_in_bf16.shape, const),
                  pl.BlockSpec((SHORT_CONV_WIDTH, w), const),
                  pl.BlockSpec((1, w), const),
                  _resident(wg.shape, lambda i: (0, 0, 0)),
                  pl.BlockSpec((1, w), const),
                  pl.BlockSpec((1, w), const),
                  pl.BlockSpec((1, w), const),
                  pl.BlockSpec((nb, w), const)],
        out_specs=[pl.BlockSpec((r, d), cur),
                   pl.BlockSpec((r, w), cur),
                   pl.BlockSpec((r, w), cur),
                   pl.BlockSpec((r, w), cur),
                   pl.BlockSpec((r, w), delayed),
                   pl.BlockSpec((r, w), delayed),
                   pl.BlockSpec((nb, w), const)],
        out_shape=[jax.ShapeDtypeStruct((n, d), F32), act, act, act, act,
                   jax.ShapeDtypeStruct((n, w), F32),
                   jax.ShapeDtypeStruct((nb, w), F32)],
        scratch_shapes=[pltpu.VMEM((lead + 2 * r, w), F32),
                        pltpu.VMEM((r, w), F32),
                        pltpu.VMEM((r, w), F32),
                        pltpu.VMEM((nb, w), F32)],
        compiler_params=_params(1),
        name="even_in",
    )(x, norm_g.reshape(1, d), shift, scale, w_in_bf16, sconv_w, sconv_b.reshape(1, w), wg,
      b_r.reshape(1, w), b_i.reshape(1, w), lam.reshape(1, w), h0)


def _even_out_kernel(uprev_ref, ucur_ref, unext_ref, sga_ref, sgb_ref, hf_ref, x_ref, gate_ref,
                     cw_ref, cb_ref, lg_ref, lb_ref, wo_ref, sc_ref, wg_ref, br_ref, bi_ref,
                     lam_ref, h0_ref, o_ref, ue_s, v_s, wrep_s, a_s, b_s, h_s, out_s,
                     *, nb, row_tile):
    i = pl.program_id(0)
    nc = pl.num_programs(0)
    c = nc - 1 - i
    r, w = ucur_ref.shape
    d = o_ref.shape[-1]
    halo = CONV_HALF * nb
    n_lane_tiles = w // LANES

    @pl.when(i == 0)
    def _():
        h_s[...] = h0_ref[...]
        for k in range(CONV_WIDTH):
            wrep_s[k] = jnp.broadcast_to(cw_ref[k:k + 1, :], (SUBLANES, w))

    def copy_in(dst_rows, src_ref):
        for lt in range(n_lane_tiles):
            ue_s[lt, dst_rows] = src_ref[:, lt * LANES:(lt + 1) * LANES].astype(F32)

    def fill(dst_rows, src_ref, valid):
        pl.when(valid)(lambda: copy_in(dst_rows, src_ref))

        @pl.when(jnp.logical_not(valid))
        def _():
            for lt in range(n_lane_tiles):
                ue_s[lt, dst_rows] = jnp.zeros((r, LANES), F32)

    fill(slice(0, r), uprev_ref, c > 0)
    copy_in(slice(r, 2 * r), ucur_ref)
    fill(slice(2 * r, 3 * r), unext_ref, c < nc - 1)

    n_row_tiles = r // row_tile
    sub = row_tile // SUBLANES
    for lt in range(n_lane_tiles):
        ls = slice(lt * LANES, (lt + 1) * LANES)
        bias = jnp.broadcast_to(cb_ref[:, ls], (SUBLANES, LANES))[None]

        def body(j, carry, lt=lt, ls=ls, bias=bias):
            base = pl.multiple_of(j * row_tile, row_tile) + (r - halo)
            def term(k):
                tap = ue_s[lt, pl.ds(base + k * nb, row_tile), :].reshape(sub, SUBLANES, LANES)
                return wrep_s[k, :, ls][None] * tap

            acc = jnp.broadcast_to(bias, (sub, SUBLANES, LANES))
            parked = None
            for k in range(CONV_WIDTH):
                acc = acc + term(k)
                if k % TAPS_PER_FENCE == TAPS_PER_FENCE - 1 and k < CONV_WIDTH - 1:
                    if parked is not None:
                        ue_s[lt, 3 * r:3 * r + row_tile, :] = parked.reshape(row_tile, LANES)
                    parked = acc
            v_s[lt, pl.ds(pl.multiple_of(j * row_tile, row_tile), row_tile), :] = (
                acc.reshape(row_tile, LANES))
            return carry

        lax.fori_loop(0, n_row_tiles, body, 0)

    v = jnp.concatenate([v_s[lt] for lt in range(n_lane_tiles)], axis=-1)
    mu = jnp.mean(v, axis=-1, keepdims=True)
    vc = v - mu
    var = jnp.mean(vc * vc, axis=-1, keepdims=True)
    yn = vc * lax.rsqrt(var + LN_EPS) * lg_ref[...] + lb_ref[...]
    ya = (_silu(yn) * sga_ref[...].astype(F32)).astype(BF16)

    q = d // (N_LRU_HEADS // 2)

    def out_slice(h):
        if h % 2 == 1:
            cols = slice((h // 2) * q, (h // 2 + 1) * q)
            out_s[:, cols] = jnp.dot(ya, wo_ref[0:w, cols], preferred_element_type=F32)

    _lru_coeffs(lambda sl: sc_ref[:, sl].astype(F32), wg_ref, br_ref, bi_ref, lam_ref, a_s, b_s,
                between_heads=out_slice)
    h_s[...] = _lru_steps(a_s, b_s, h_s[...], b_s, r // nb, nb, reverse=True)

    yb = ((hf_ref[...] + b_s[...]) * sgb_ref[...].astype(F32)).astype(BF16)
    out = out_s[...] + jnp.dot(yb, wo_ref[w:2 * w, :], preferred_element_type=F32)
    x = x_ref[...].reshape(r // nb, nb, d)
    o_ref[...] = (x + gate_ref[...][None] * out.reshape(r // nb, nb, d)).reshape(r, d)


def _even_out(u, sga, sgb, hf, sc, rows, gate, conv_w, conv_b, ln_g, ln_b, w_out_bf16,
              wg, b_r, b_i, lam, h0, *, steps):
    n, w = u.shape
    d = rows.shape[1]
    nb = gate.shape[0]
    r = steps * nb
    nc = n // r
    assert steps >= CONV_HALF
    const = lambda i: (0, 0)
    chunk = lambda i: nc - 1 - i
    tile = lambda i: (chunk(i), 0)
    row_tile = 64
    kern = functools.partial(_even_out_kernel, nb=nb, row_tile=row_tile)
    return pl.pallas_call(
        kern,
        grid=(nc,),
        in_specs=[
            pl.BlockSpec((r, w), lambda i: (jnp.maximum(chunk(i) - 1, 0), 0)),
            pl.BlockSpec((r, w), tile),
            pl.BlockSpec((r, w), lambda i: (jnp.minimum(chunk(i) + 1, nc - 1), 0)),
            pl.BlockSpec((r, w), tile),
            pl.BlockSpec((r, w), tile),
            pl.BlockSpec((r, w), tile),
            pl.BlockSpec((r, d), tile),
            pl.BlockSpec((nb, d), const),
            pl.BlockSpec((CONV_WIDTH, w), const),
            pl.BlockSpec((1, w), const),
            pl.BlockSpec((1, w), const),
            pl.BlockSpec((1, w), const),
            _resident(w_out_bf16.shape, const),
            pl.BlockSpec((r, w), tile),
            _resident(wg.shape, lambda i: (0, 0, 0)),
            pl.BlockSpec((1, w), const),
            pl.BlockSpec((1, w), const),
            pl.BlockSpec((1, w), const),
            pl.BlockSpec((nb, w), const),
        ],
        out_specs=pl.BlockSpec((r, d), tile),
        out_shape=jax.ShapeDtypeStruct((n, d), F32),
        scratch_shapes=[pltpu.VMEM((w // LANES, 3 * r + row_tile, LANES), F32),
                        pltpu.VMEM((w // LANES, r, LANES), F32),
                        pltpu.VMEM((CONV_WIDTH, SUBLANES, w), F32),
                        pltpu.VMEM((r, w), F32),
                        pltpu.VMEM((r, w), F32),
                        pltpu.VMEM((nb, w), F32),
                        pltpu.VMEM((r, d), F32)],
        compiler_params=_params(1),
        name="even_out",
    )(u, u, u, sga, sgb, hf, rows, gate, conv_w, conv_b.reshape(1, w), ln_g.reshape(1, w),
      ln_b.reshape(1, w), w_out_bf16, sc, wg, b_r.reshape(1, w), b_i.reshape(1, w),
      lam.reshape(1, w), h0)


def _odd_kernel(x_ref, ng_ref, shift_ref, scale_ref, gate_ref, win_ref, wgrp_ref, pscale_ref,
                wo_ref, fg_ref, o_ref, ue_s, inv_s, *, nbb):
    first = (pl.program_id(0) == 0) & (pl.program_id(1) == 0)
    gw, _, d = x_ref.shape
    n_groups = len(POOL_WINDOWS)
    wp = pscale_ref.shape[-1]
    gd = wp // n_groups
    r = gw * nbb
    pad = (max(POOL_WINDOWS) // 2) * nbb

    @pl.when(first)
    def _():
        t = lax.broadcasted_iota(jnp.int32, (r, LANES), 0) // nbb
        for gi, win in enumerate(POOL_WINDOWS):
            half = win // 2
            cnt = jnp.minimum(t + half, gw) - jnp.maximum(t - half, 0)
            inv_s[gi] = 1.0 / cnt.astype(F32)
        zeros = jnp.zeros((pad, gd), F32)
        for buf in range(ue_s.shape[0]):
            ue_s[buf, 0:pad] = zeros
            ue_s[buf, pad + r:pad + r + pad] = zeros

    x = x_ref[...]
    hx = _modulated_rmsnorm(x.reshape(r, d), ng_ref[...], shift_ref[...], scale_ref[...],
                            nbb).astype(BF16)
    acc = None
    for gi, win in enumerate(POOL_WINDOWS):
        gs = slice(gi * gd, (gi + 1) * gd)
        buf = gi % ue_s.shape[0]
        ue_s[buf, pad:pad + r] = jnp.dot(hx, win_ref[:, gs], preferred_element_type=F32)
        sgate = _silu(jnp.dot(hx, win_ref[:, wp + gi * gd:wp + (gi + 1) * gd],
                              preferred_element_type=F32))
        e = ue_s[buf]
        cur = e[0:r + 2 * pad - nbb] + e[nbb:r + 2 * pad]
        lo = 1
        span = 2
        while span < win:
            sh = (span // 2) * nbb
            cur = cur[0:cur.shape[0] - 2 * sh] + cur[2 * sh:]
            lo += span // 2
            span *= 2
        start = pad - lo * nbb
        inv = jnp.concatenate([inv_s[gi]] * (gd // LANES), axis=-1)
        dlt = (cur[start:start + r] * inv - ue_s[buf, pad:pad + r]).astype(BF16)
        y = jnp.dot(dlt, wgrp_ref[gi], preferred_element_type=F32)
        y = y * pscale_ref[:, gs] * sgate
        part = jnp.dot(y.astype(BF16), wo_ref[gs, :], preferred_element_type=F32)
        acc = part if acc is None else acc + part

    xo = x + gate_ref[...][None] * acc.reshape(gw, nbb, d)
    ms = jnp.mean(xo * xo, axis=-1, keepdims=True)
    o_ref[...] = jnp.swapaxes(xo * lax.rsqrt(ms + RMS_EPS) * fg_ref[...][None], 0, 1)


def _odd_layer(rows3, norm_g, shift, scale, gate, w_in_bf16, w_grp_bf16, pscale, w_out_bf16,
               final_g, *, nbb):
    seq, nb, d = rows3.shape
    n_groups = len(POOL_WINDOWS)
    wp = pscale.shape[-1]
    gd = wp // n_groups
    pad = (max(POOL_WINDOWS) // 2) * nbb
    r = GRID_W * nbb
    blk = lambda i, j: (i, j, 0)
    per_batch = lambda i, j: (j, 0)
    const = lambda i, j: (0, 0)
    kern = functools.partial(_odd_kernel, nbb=nbb)
    return pl.pallas_call(
        kern,
        grid=(seq // GRID_W, nb // nbb),
        in_specs=[
            pl.BlockSpec((GRID_W, nbb, d), blk),
            pl.BlockSpec((1, d), const),
            pl.BlockSpec((nbb, d), per_batch),
            pl.BlockSpec((nbb, d), per_batch),
            pl.BlockSpec((nbb, d), per_batch),
            _resident(w_in_bf16.shape, const),
            _resident(w_grp_bf16.shape, lambda i, j: (0, 0, 0)),
            pl.BlockSpec((1, wp), const),
            _resident(w_out_bf16.shape, const),
            pl.BlockSpec((1, d), const),
        ],
        out_specs=pl.BlockSpec((nbb, GRID_W, d), lambda i, j: (j, i, 0)),
        out_shape=jax.ShapeDtypeStruct((nb, seq, d), F32),
        scratch_shapes=[pltpu.VMEM((2, r + 2 * pad, gd), F32),
                        pltpu.VMEM((n_groups, r, LANES), F32)],
        compiler_params=_params(2),
        name="odd_layer",
    )(rows3, norm_g.reshape(1, d), shift, scale, gate, w_in_bf16, w_grp_bf16,
      pscale.reshape(1, wp), w_out_bf16, final_g.reshape(1, d))


def kernel(x, c, ctx, c_ctx, norm_g, mod_w, mod_b, ev_w_in, ev_conv_w, ev_conv_b, ev_ln_g, ev_ln_b,
           ev_sconv_w, ev_sconv_b, ev_w_r, ev_b_r, ev_w_i, ev_b_i, ev_lam, ev_w_out,
           od_w_in, od_w_grp, od_scale, od_w_out, final_g):
    bn, seq, d = x.shape
    n_ctx = ctx.shape[1]
    w_conv = ev_conv_w.shape[-1]
    w_lru = ev_sconv_w.shape[-1]
    assert w_conv == w_lru

    n_cond = -(-(bn + 1) // SUBLANES) * SUBLANES
    cc = jnp.zeros((n_cond, d), F32).at[:bn].set(c).at[bn].set(c_ctx)
    mod = _modulation(cc, mod_w, mod_b)
    shift0, scale0, gate0 = (mod[0, :bn, k * d:(k + 1) * d] for k in range(3))
    shift1, scale1, gate1 = (mod[1, :bn, k * d:(k + 1) * d] for k in range(3))
    shift_c = jnp.broadcast_to(mod[0, bn:bn + 1, 0:d], (bn, d))
    scale_c = jnp.broadcast_to(mod[0, bn:bn + 1, d:2 * d], (bn, d))

    w_in = ev_w_in[0].astype(BF16)
    xb_col = 3 * w_conv
    xb_ctx = _ctx_inproj(ctx, norm_g[0], shift_c, scale_c, w_in[:, xb_col:xb_col + w_lru], 16)

    def gate_params(k):
        wg = (0.5 * jnp.concatenate([ev_w_r[0, k], ev_w_i[0, k]], axis=-1)).astype(BF16)
        return dict(wg=wg, b_r=ev_b_r[0, k], b_i=ev_b_i[0, k], lam=ev_lam[0, k])

    sconv = dict(sconv_w=ev_sconv_w[0], sconv_b=ev_sconv_b[0])
    zero_state = jnp.zeros((bn, w_lru), F32)
    _, h_ctx_f = _lru_scan(xb_ctx, h0=zero_state, steps=16, reverse=False, name="ctx_scan0",
                           **sconv, **gate_params(0))
    _, h_ctx_b = _lru_scan(xb_ctx, h0=zero_state, steps=16, reverse=True, name="ctx_scan1",
                           **sconv, **gate_params(1))

    rows, u, sga, sgb, sc, hf, _ = _even_in(x, norm_g[0], shift0, scale0, w_in, h0=h_ctx_f,
                                            steps=16, **sconv, **gate_params(0))
    rows = _even_out(u, sga, sgb, hf, sc, rows, gate0, ev_conv_w[0], ev_conv_b[0],
                     ev_ln_g[0], ev_ln_b[0], ev_w_out[0].astype(BF16), h0=h_ctx_b, steps=16,
                     **gate_params(1))

    return _odd_layer(rows.reshape(seq, bn, d), norm_g[1], shift1, scale1, gate1,
                      od_w_in[0].astype(BF16), od_w_grp[0].astype(BF16), od_scale[0],
                      od_w_out[0].astype(BF16), final_g, nbb=16)
```

```python
import functools

import jax
import jax.numpy as jnp
from jax import lax
from jax.experimental import pallas as pl
from jax.experimental.pallas import tpu as pltpu

F32 = jnp.float32
BF16 = jnp.bfloat16

RMS_EPS = 1e-6
LN_EPS = 1e-5
LRU_C = 8.0
LOG2E = 1.4426950408889634
N_LRU_HEADS = 8
CONV_WIDTH = 31
CONV_HALF = 15
SHORT_CONV_WIDTH = 4
SHORT_CONV_LEAD = 2
TAPS_PER_FENCE = 6
GRID_W = 64
POOL_WINDOWS = (2, 4, 8, 16)

LANES = 128
SUBLANES = 8
VMEM_LIMIT_BYTES = 56 * 1024 * 1024


def _params(n_axes):
    return pltpu.CompilerParams(
        dimension_semantics=("arbitrary",) * n_axes,
        vmem_limit_bytes=VMEM_LIMIT_BYTES)


def _resident(shape, index_map):
    return pl.BlockSpec(shape, index_map, pipeline_mode=pl.Buffered(1))


def _sigmoid(x):
    return 0.5 * jnp.tanh(0.5 * x) + 0.5


def _silu(x):
    return x * _sigmoid(x)


def _modulated_rmsnorm(x, g, shift, scale, nb):
    rows, d = x.shape
    ms = jnp.mean(x * x, axis=-1, keepdims=True)
    y = x * lax.rsqrt(ms + RMS_EPS) * g
    y = y.reshape(rows // nb, nb, d) * (1.0 + scale)[None] + shift[None]
    return y.reshape(rows, d)


def _mod_kernel(c_ref, w_ref, b_ref, o_ref):
    s = _silu(c_ref[...])
    o_ref[0] = jnp.dot(s, w_ref[0], preferred_element_type=F32,
                       precision=lax.Precision.HIGHEST) + b_ref[0]


def _modulation(cc, mod_w, mod_b):
    depth, d, d3 = mod_w.shape
    r = cc.shape[0]
    tn = 1024
    return pl.pallas_call(
        _mod_kernel,
        grid=(depth, d3 // tn),
        in_specs=[pl.BlockSpec((r, d), lambda l, j: (0, 0)),
                  pl.BlockSpec((1, d, tn), lambda l, j: (l, 0, j)),
                  pl.BlockSpec((1, 1, tn), lambda l, j: (l, 0, j))],
        out_specs=pl.BlockSpec((1, r, tn), lambda l, j: (l, 0, j)),
        out_shape=jax.ShapeDtypeStruct((depth, r, d3), F32),
        compiler_params=_params(2),
        name="modulation",
    )(cc, mod_w, mod_b.reshape(depth, 1, d3))


def _lru_rate(lam):
    neg = -lam
    return LRU_C * (jnp.maximum(neg, 0.0) + jnp.log1p(jnp.exp(-jnp.abs(neg))))


def _short_conv(xe_s, r, nb, cw_ref, cb_ref, sl):
    u = cb_ref[:, sl] + cw_ref[0:1, sl] * xe_s[0:r, sl]
    for k in range(1, SHORT_CONV_WIDTH):
        u = u + cw_ref[k:k + 1, sl] * xe_s[k * nb:k * nb + r, sl]
    return u


def _lru_coeffs(conv_out, wg_ref, br_ref, bi_ref, lam_ref, a_s, b_s, between_heads=None):
    w = a_s.shape[-1]
    rate = _lru_rate(lam_ref[...])
    neg_rate_log2e = rate * (-LOG2E)
    half_br = 0.5 * br_ref[...]
    half_bi = 0.5 * bi_ref[...]
    hd = w // N_LRU_HEADS
    for h in range(N_LRU_HEADS):
        if between_heads is not None and h > 0:
            between_heads(h - 1)
        sl = slice(h * hd, (h + 1) * hd)
        u = conv_out(sl)
        g = jnp.dot(u.astype(BF16), wg_ref[h], preferred_element_type=F32)
        rg = 0.5 * jnp.tanh(g[:, :hd] + half_br[:, sl]) + 0.5
        ig = 0.5 * jnp.tanh(g[:, hd:] + half_bi[:, sl]) + 0.5
        p = rg * rate[:, sl]
        a = jnp.exp2(rg * neg_rate_log2e[:, sl])
        a_s[:, sl] = a
        y = jnp.tanh(p) * (a * a + 1.0)
        b_s[:, sl] = jnp.exp2(0.5 * jnp.log2(y)) * (ig * u)
    if between_heads is not None:
        between_heads(N_LRU_HEADS - 1)


def _lru_steps(a_s, b_s, h, hs_ref, steps, nb, reverse):
    def step(s, hcur):
        t = steps - 1 - s if reverse else s
        rows = pl.ds(pl.multiple_of(t * nb, nb), nb)
        hnew = a_s[rows, :] * hcur + b_s[rows, :]
        hs_ref[rows, :] = hnew
        return hnew

    return lax.fori_loop(0, steps, step, h)


def _lru_kernel(prev_ref, cur_ref, next_ref, cw_ref, cb_ref, wg_ref, br_ref, bi_ref,
                lam_ref, h0_ref, hs_ref, ht_ref, xe_s, a_s, b_s, h_s, *, nb, reverse):
    i = pl.program_id(0)
    nc = pl.num_programs(0)
    c = nc - 1 - i if reverse else i
    r, w = cur_ref.shape
    lead = SHORT_CONV_LEAD * nb

    @pl.when(i == 0)
    def _():
        h_s[...] = h0_ref[...]

    @pl.when(c > 0)
    def _():
        xe_s[0:lead] = prev_ref[...].astype(F32)

    @pl.when(c == 0)
    def _():
        xe_s[0:lead] = jnp.zeros((lead, w), F32)

    xe_s[lead:lead + r] = cur_ref[...].astype(F32)

    @pl.when(c < nc - 1)
    def _():
        xe_s[lead + r:lead + r + nb] = next_ref[...].astype(F32)

    @pl.when(c == nc - 1)
    def _():
        xe_s[lead + r:lead + r + nb] = jnp.zeros((nb, w), F32)

    _lru_coeffs(lambda sl: _short_conv(xe_s, r, nb, cw_ref, cb_ref, sl),
                wg_ref, br_ref, bi_ref, lam_ref, a_s, b_s)
    hfin = _lru_steps(a_s, b_s, h_s[...], hs_ref, r // nb, nb, reverse)
    h_s[...] = hfin
    ht_ref[...] = hfin


def _lru_scan(xb, sconv_w, sconv_b, wg, b_r, b_i, lam, h0, *, steps, reverse, name):
    n, w = xb.shape
    nb = h0.shape[0]
    r = steps * nb
    nc = n // r
    lead = SHORT_CONV_LEAD * nb
    chunk = (lambda i: nc - 1 - i) if reverse else (lambda i: i)
    const = lambda i: (0, 0)
    kern = functools.partial(_lru_kernel, nb=nb, reverse=reverse)
    return pl.pallas_call(
        kern,
        grid=(nc,),
        in_specs=[
            pl.BlockSpec((lead, w), lambda i: (jnp.maximum(chunk(i) * (r // lead) - 1, 0), 0)),
            pl.BlockSpec((r, w), lambda i: (chunk(i), 0)),
            pl.BlockSpec((nb, w), lambda i: (jnp.minimum((chunk(i) + 1) * (r // nb), n // nb - 1), 0)),
            pl.BlockSpec((SHORT_CONV_WIDTH, w), const),
            pl.BlockSpec((1, w), const),
            pl.BlockSpec(wg.shape, lambda i: (0, 0, 0)),
            pl.BlockSpec((1, w), const),
            pl.BlockSpec((1, w), const),
            pl.BlockSpec((1, w), const),
            pl.BlockSpec((nb, w), const),
        ],
        out_specs=[pl.BlockSpec((r, w), lambda i: (chunk(i), 0)),
                   pl.BlockSpec((nb, w), const)],
        out_shape=[jax.ShapeDtypeStruct((n, w), F32),
                   jax.ShapeDtypeStruct((nb, w), F32)],
        scratch_shapes=[pltpu.VMEM((lead + r + nb, w), F32),
                        pltpu.VMEM((r, w), F32),
                        pltpu.VMEM((r, w), F32),
                        pltpu.VMEM((nb, w), F32)],
        compiler_params=_params(1),
        name=name,
    )(xb, xb, xb, sconv_w, sconv_b.reshape(1, w), wg, b_r.reshape(1, w), b_i.reshape(1, w),
      lam.reshape(1, w), h0)


def _time_major(x_ref):
    nb, steps, d = x_ref.shape
    return jnp.swapaxes(x_ref[...], 0, 1).reshape(steps * nb, d)


def _ctx_inproj_kernel(x_ref, g_ref, shift_ref, scale_ref, w_ref, o_ref, *, nb):
    hx = _modulated_rmsnorm(_time_major(x_ref), g_ref[...], shift_ref[...], scale_ref[...],
                            nb).astype(BF16)
    o_ref[...] = jnp.dot(hx, w_ref[...], preferred_element_type=F32).astype(o_ref.dtype)


def _ctx_inproj(x, norm_g, shift, scale, w_bf16, steps):
    nb, seq, d = x.shape
    n = seq * nb
    tm = steps * nb
    width = w_bf16.shape[1]
    const = lambda i: (0, 0)
    return pl.pallas_call(
        functools.partial(_ctx_inproj_kernel, nb=nb),
        grid=(n // tm,),
        in_specs=[pl.BlockSpec((nb, steps, d), lambda i: (0, i, 0)),
                  pl.BlockSpec((1, d), const),
                  pl.BlockSpec((nb, d), const),
                  pl.BlockSpec((nb, d), const),
                  _resident(w_bf16.shape, const)],
        out_specs=pl.BlockSpec((tm, width), lambda i: (i, 0)),
        out_shape=jax.ShapeDtypeStruct((n, width), BF16),
        compiler_params=_params(1),
        name="ctx_inproj",
    )(x, norm_g.reshape(1, d), shift, scale, w_bf16)


def _even_in_kernel(x_ref, g_ref, shift_ref, scale_ref, w_ref, cw_ref, cb_ref, wg_ref, br_ref,
                    bi_ref, lam_ref, h0_ref, rows_ref, u_ref, sga_ref, sgb_ref, sc_ref, hs_ref,
                    ht_ref, xe_s, a_s, b_s, h_s, *, nb):
    c = pl.program_id(0)
    nc = pl.num_programs(0) - 1
    r, w = u_ref.shape
    lead = SHORT_CONV_LEAD * nb

    @pl.when(c == 0)
    def _():
        h_s[...] = h0_ref[...]
        xe_s[...] = jnp.zeros(xe_s.shape, F32)

    xe_s[0:lead + r] = xe_s[r:lead + 2 * r]

    x = _time_major(x_ref)
    rows_ref[...] = x
    hx = _modulated_rmsnorm(x, g_ref[...], shift_ref[...], scale_ref[...], nb).astype(BF16)

    def proj(group, cols=slice(0, w)):
        return jnp.dot(hx, w_ref[:, group * w + cols.start:group * w + cols.stop],
                       preferred_element_type=F32)

    xe_s[lead + r:lead + 2 * r] = proj(3) * (c < nc).astype(F32)

    q = w // (N_LRU_HEADS // 2)

    def proj_slice(i):
        if i < N_LRU_HEADS // 2:
            cols = slice(i * q, (i + 1) * q)
            u_ref[:, cols] = (proj(0, cols) * _sigmoid(proj(1, cols))).astype(u_ref.dtype)
        else:
            j = i - N_LRU_HEADS // 2
            o_ref, group = (sga_ref, 2) if j < N_LRU_HEADS // 4 else (sgb_ref, 4)
            cols = slice((j % (N_LRU_HEADS // 4)) * 2 * q, (j % (N_LRU_HEADS // 4) + 1) * 2 * q)
            o_ref[:, cols] = _silu(proj(group, cols)).astype(o_ref.dtype)

    def conv_out(sl):
        uc = _short_conv(xe_s, r, nb, cw_ref, cb_ref, sl)
        sc_ref[:, sl] = uc.astype(sc_ref.dtype)
        return uc

    _lru_coeffs(conv_out, wg_ref, br_ref, bi_ref, lam_ref, a_s, b_s, between_heads=proj_slice)

    h_prev = h_s[...]
    hfin = _lru_steps(a_s, b_s, h_prev, hs_ref, r // nb, nb, reverse=False)
    hfin = jnp.where(c > 0, hfin, h_prev)
    h_s[...] = hfin
    ht_ref[...] = hfin


def _even_in(x, norm_g, shift, scale, w_in_bf16, sconv_w, sconv_b, wg, b_r, b_i, lam, h0,
             *, steps):
    nb, seq, d = x.shape
    n = seq * nb
    w = sconv_w.shape[-1]
    r = steps * nb
    nc = n // r
    lead = SHORT_CONV_LEAD * nb
    assert w_in_bf16.shape == (d, 5 * w)
    const = lambda i: (0, 0)
    cur = lambda i: (jnp.minimum(i, nc - 1), 0)
    delayed = lambda i: (jnp.maximum(i - 1, 0), 0)
    kern = functools.partial(_even_in_kernel, nb=nb)
    act = jax.ShapeDtypeStruct((n, w), BF16)
    return pl.pallas_call(
        kern,
        grid=(nc + 1,),
        in_specs=[pl.BlockSpec((nb, steps, d), lambda i: (0, jnp.minimum(i, nc - 1), 0)),
                  pl.BlockSpec((1, d), const),
                  pl.BlockSpec((nb, d), const),
                  pl.BlockSpec((nb, d), const),
                  _resident(w_in_bf16.shape, const),
                  pl.BlockSpec((SHORT_CONV_WIDTH, w), const),
                  pl.BlockSpec((1, w), const),
                  _resident(wg.shape, lambda i: (0, 0, 0)),
                  pl.BlockSpec((1, w), const),
                  pl.BlockSpec((1, w), const),
                  pl.BlockSpec((1, w), const),
                  pl.BlockSpec((nb, w), const)],
        out_specs=[pl.BlockSpec((r, d), cur),
                   pl.BlockSpec((r, w), cur),
                   pl.BlockSpec((r, w), cur),
                   pl.BlockSpec((r, w), cur),
                   pl.BlockSpec((r, w), delayed),
                   pl.BlockSpec((r, w), delayed),
                   pl.BlockSpec((nb, w), const)],
        out_shape=[jax.ShapeDtypeStruct((n, d), F32), act, act, act, act,
                   jax.ShapeDtypeStruct((n, w), F32),
                   jax.ShapeDtypeStruct((nb, w), F32)],
        scratch_shapes=[pltpu.VMEM((lead + 2 * r, w), F32),
                        pltpu.VMEM((r, w), F32),
                        pltpu.VMEM((r, w), F32),
                        pltpu.VMEM((nb, w), F32)],
        compiler_params=_params(1),
        name="even_in",
    )(x, norm_g.reshape(1, d), shift, scale, w_in_bf16, sconv_w, sconv_b.reshape(1, w), wg,
      b_r.reshape(1, w), b_i.reshape(1, w), lam.reshape(1, w), h0)


def _even_out_kernel(ulast_ref, ulast2_ref, uahead_ref, sga_ref, sgb_ref, hf_ref, x_ref, gate_ref,
                     cw_ref, cb_ref, lg_ref, lb_ref, wo_ref, sc_ref, wg_ref, br_ref, bi_ref,
                     lam_ref, h0_ref, o_ref, ue_s, v_s, wrep_s, a_s, b_s, h_s, out_s,
                     *, nb, row_tile):
    i = pl.program_id(0)
    nc = pl.num_programs(0)
    c = nc - 1 - i
    r, w = sga_ref.shape
    d = o_ref.shape[-1]
    halo = CONV_HALF * nb
    n_lane_tiles = w // LANES

    def lane_tile(ref, lt):
        return ref[:, lt * LANES:(lt + 1) * LANES].astype(F32)

    @pl.when(i == 0)
    def _():
        h_s[...] = h0_ref[...]
        for k in range(CONV_WIDTH):
            wrep_s[k] = jnp.broadcast_to(cw_ref[k:k + 1, :], (SUBLANES, w))
        for lt in range(n_lane_tiles):
            ue_s[lt, 0:r] = lane_tile(ulast2_ref, lt)
            ue_s[lt, r:2 * r] = lane_tile(ulast_ref, lt)
            ue_s[lt, 2 * r:3 * r] = jnp.zeros((r, LANES), F32)

    def shift_in(lt):
        ahead_valid = (c >= 2).astype(F32)
        ue_s[lt, 2 * r:3 * r] = ue_s[lt, r:2 * r]
        ue_s[lt, r:2 * r] = ue_s[lt, 0:r]
        ue_s[lt, 0:r] = lane_tile(uahead_ref, lt) * ahead_valid

    n_row_tiles = r // row_tile
    sub = row_tile // SUBLANES
    for lt in range(n_lane_tiles):
        ls = slice(lt * LANES, (lt + 1) * LANES)
        bias = jnp.broadcast_to(cb_ref[:, ls], (SUBLANES, LANES))[None]

        def body(j, carry, lt=lt, ls=ls, bias=bias):
            base = pl.multiple_of(j * row_tile, row_tile) + (r - halo)
            def term(k):
                tap = ue_s[lt, pl.ds(base + k * nb, row_tile), :].reshape(sub, SUBLANES, LANES)
                return wrep_s[k, :, ls][None] * tap

            acc = jnp.broadcast_to(bias, (sub, SUBLANES, LANES))
            parked = None
            for k in range(CONV_WIDTH):
                acc = acc + term(k)
                if k % TAPS_PER_FENCE == TAPS_PER_FENCE - 1 and k < CONV_WIDTH - 1:
                    if parked is not None:
                        ue_s[lt, 3 * r:3 * r + row_tile, :] = parked.reshape(row_tile, LANES)
                    parked = acc
            v_s[lt, pl.ds(pl.multiple_of(j * row_tile, row_tile), row_tile), :] = (
                acc.reshape(row_tile, LANES))
            return carry

        lax.fori_loop(0, n_row_tiles, body, 0)

    v = jnp.concatenate([v_s[lt] for lt in range(n_lane_tiles)], axis=-1)
    mu = jnp.mean(v, axis=-1, keepdims=True)
    vc = v - mu
    var = jnp.mean(vc * vc, axis=-1, keepdims=True)
    yn = vc * lax.rsqrt(var + LN_EPS) * lg_ref[...] + lb_ref[...]
    ya = (_silu(yn) * sga_ref[...].astype(F32)).astype(BF16)

    q = d // (N_LRU_HEADS // 2)

    def after_head(h):
        for lt in range(h * n_lane_tiles // N_LRU_HEADS, (h + 1) * n_lane_tiles // N_LRU_HEADS):
            shift_in(lt)
        if h % 2 == 1:
            cols = slice((h // 2) * q, (h // 2 + 1) * q)
            out_s[:, cols] = jnp.dot(ya, wo_ref[0:w, cols], preferred_element_type=F32)

    _lru_coeffs(lambda sl: sc_ref[:, sl].astype(F32), wg_ref, br_ref, bi_ref, lam_ref, a_s, b_s,
                between_heads=after_head)
    h_s[...] = _lru_steps(a_s, b_s, h_s[...], b_s, r // nb, nb, reverse=True)

    yb = ((hf_ref[...] + b_s[...]) * sgb_ref[...].astype(F32)).astype(BF16)
    out = out_s[...] + jnp.dot(yb, wo_ref[w:2 * w, :], preferred_element_type=F32)
    x = x_ref[...].reshape(r // nb, nb, d)
    o_ref[...] = (x + gate_ref[...][None] * out.reshape(r // nb, nb, d)).reshape(r, d)


def _even_out(u, sga, sgb, hf, sc, rows, gate, conv_w, conv_b, ln_g, ln_b, w_out_bf16,
              wg, b_r, b_i, lam, h0, *, steps):
    n, w = u.shape
    d = rows.shape[1]
    nb = gate.shape[0]
    r = steps * nb
    nc = n // r
    assert steps >= CONV_HALF
    const = lambda i: (0, 0)
    chunk = lambda i: nc - 1 - i
    tile = lambda i: (chunk(i), 0)
    row_tile = 64
    kern = functools.partial(_even_out_kernel, nb=nb, row_tile=row_tile)
    return pl.pallas_call(
        kern,
        grid=(nc,),
        in_specs=[
            pl.BlockSpec((r, w), lambda i: (nc - 1, 0)),
            pl.BlockSpec((r, w), lambda i: (nc - 2, 0)),
            pl.BlockSpec((r, w), lambda i: (jnp.maximum(chunk(i) - 2, 0), 0)),
            pl.BlockSpec((r, w), tile),
            pl.BlockSpec((r, w), tile),
            pl.BlockSpec((r, w), tile),
            pl.BlockSpec((r, d), tile),
            pl.BlockSpec((nb, d), const),
            pl.BlockSpec((CONV_WIDTH, w), const),
            pl.BlockSpec((1, w), const),
            pl.BlockSpec((1, w), const),
            pl.BlockSpec((1, w), const),
            _resident(w_out_bf16.shape, const),
            pl.BlockSpec((r, w), tile),
            _resident(wg.shape, lambda i: (0, 0, 0)),
            pl.BlockSpec((1, w), const),
            pl.BlockSpec((1, w), const),
            pl.BlockSpec((1, w), const),
            pl.BlockSpec((nb, w), const),
        ],
        out_specs=pl.BlockSpec((r, d), tile),
        out_shape=jax.ShapeDtypeStruct((n, d), F32),
        scratch_shapes=[pltpu.VMEM((w // LANES, 3 * r + row_tile, LANES), F32),
                        pltpu.VMEM((w // LANES, r, LANES), F32),
                        pltpu.VMEM((CONV_WIDTH, SUBLANES, w), F32),
                        pltpu.VMEM((r, w), F32),
                        pltpu.VMEM((r, w), F32),
                        pltpu.VMEM((nb, w), F32),
                        pltpu.VMEM((r, d), F32)],
        compiler_params=_params(1),
        name="even_out",
    )(u, u, u, sga, sgb, hf, rows, gate, conv_w, conv_b.reshape(1, w), ln_g.reshape(1, w),
      ln_b.reshape(1, w), w_out_bf16, sc, wg, b_r.reshape(1, w), b_i.reshape(1, w),
      lam.reshape(1, w), h0)


def _odd_kernel(x_ref, ng_ref, shift_ref, scale_ref, gate_ref, win_ref, wgrp_ref, pscale_ref,
                wo_ref, fg_ref, o_ref, ue_s, inv_s, *, nbb):
    first = (pl.program_id(0) == 0) & (pl.program_id(1) == 0)
    gw, _, d = x_ref.shape
    n_groups = len(POOL_WINDOWS)
    wp = pscale_ref.shape[-1]
    gd = wp // n_groups
    r = gw * nbb
    pad = (max(POOL_WINDOWS) // 2) * nbb

    @pl.when(first)
    def _():
        t = lax.broadcasted_iota(jnp.int32, (r, LANES), 0) // nbb
        for gi, win in enumerate(POOL_WINDOWS):
            half = win // 2
            cnt = jnp.minimum(t + half, gw) - jnp.maximum(t - half, 0)
            inv_s[gi] = 1.0 / cnt.astype(F32)
        zeros = jnp.zeros((pad, gd), F32)
        for buf in range(ue_s.shape[0]):
            ue_s[buf, 0:pad] = zeros
            ue_s[buf, pad + r:pad + r + pad] = zeros

    x = x_ref[...]
    hx = _modulated_rmsnorm(x.reshape(r, d), ng_ref[...], shift_ref[...], scale_ref[...],
                            nbb).astype(BF16)
    acc = None
    for gi, win in enumerate(POOL_WINDOWS):
        gs = slice(gi * gd, (gi + 1) * gd)
        buf = gi % ue_s.shape[0]
        ue_s[buf, pad:pad + r] = jnp.dot(hx, win_ref[:, gs], preferred_element_type=F32)
        sgate = _silu(jnp.dot(hx, win_ref[:, wp + gi * gd:wp + (gi + 1) * gd],
                              preferred_element_type=F32))
        e = ue_s[buf]
        cur = e[0:r + 2 * pad - nbb] + e[nbb:r + 2 * pad]
        lo = 1
        span = 2
        while span < win:
            sh = (span // 2) * nbb
            cur = cur[0:cur.shape[0] - 2 * sh] + cur[2 * sh:]
            lo += span // 2
            span *= 2
        start = pad - lo * nbb
        inv = jnp.concatenate([inv_s[gi]] * (gd // LANES), axis=-1)
        dlt = (cur[start:start + r] * inv - ue_s[buf, pad:pad + r]).astype(BF16)
        y = jnp.dot(dlt, wgrp_ref[gi], preferred_element_type=F32)
        y = y * pscale_ref[:, gs] * sgate
        part = jnp.dot(y.astype(BF16), wo_ref[gs, :], preferred_element_type=F32)
        acc = part if acc is None else acc + part

    xo = x + gate_ref[...][None] * acc.reshape(gw, nbb, d)
    ms = jnp.mean(xo * xo, axis=-1, keepdims=True)
    o_ref[...] = jnp.swapaxes(xo * lax.rsqrt(ms + RMS_EPS) * fg_ref[...][None], 0, 1)


def _odd_layer(rows3, norm_g, shift, scale, gate, w_in_bf16, w_grp_bf16, pscale, w_out_bf16,
               final_g, *, nbb):
    seq, nb, d = rows3.shape
    n_groups = len(POOL_WINDOWS)
    wp = pscale.shape[-1]
    gd = wp // n_groups
    pad = (max(POOL_WINDOWS) // 2) * nbb
    r = GRID_W * nbb
    blk = lambda i, j: (i, j, 0)
    per_batch = lambda i, j: (j, 0)
    const = lambda i, j: (0, 0)
    kern = functools.partial(_odd_kernel, nbb=nbb)
    return pl.pallas_call(
        kern,
        grid=(seq // GRID_W, nb // nbb),
        in_specs=[
            pl.BlockSpec((GRID_W, nbb, d), blk),
            pl.BlockSpec((1, d), const),
            pl.BlockSpec((nbb, d), per_batch),
            pl.BlockSpec((nbb, d), per_batch),
            pl.BlockSpec((nbb, d), per_batch),
            _resident(w_in_bf16.shape, const),
            _resident(w_grp_bf16.shape, lambda i, j: (0, 0, 0)),
            pl.BlockSpec((1, wp), const),
            _resident(w_out_bf16.shape, const),
            pl.BlockSpec((1, d), const),
        ],
        out_specs=pl.BlockSpec((nbb, GRID_W, d), lambda i, j: (j, i, 0)),
        out_shape=jax.ShapeDtypeStruct((nb, seq, d), F32),
        scratch_shapes=[pltpu.VMEM((2, r + 2 * pad, gd), F32),
                        pltpu.VMEM((n_groups, r, LANES), F32)],
        compiler_params=_params(2),
        name="odd_layer",
    )(rows3, norm_g.reshape(1, d), shift, scale, gate, w_in_bf16, w_grp_bf16,
      pscale.reshape(1, wp), w_out_bf16, final_g.reshape(1, d))


def kernel(x, c, ctx, c_ctx, norm_g, mod_w, mod_b, ev_w_in, ev_conv_w, ev_conv_b, ev_ln_g, ev_ln_b,
           ev_sconv_w, ev_sconv_b, ev_w_r, ev_b_r, ev_w_i, ev_b_i, ev_lam, ev_w_out,
           od_w_in, od_w_grp, od_scale, od_w_out, final_g):
    bn, seq, d = x.shape
    n_ctx = ctx.shape[1]
    w_conv = ev_conv_w.shape[-1]
    w_lru = ev_sconv_w.shape[-1]
    assert w_conv == w_lru

    n_cond = -(-(bn + 1) // SUBLANES) * SUBLANES
    cc = jnp.zeros((n_cond, d), F32).at[:bn].set(c).at[bn].set(c_ctx)
    mod = _modulation(cc, mod_w, mod_b)
    shift0, scale0, gate0 = (mod[0, :bn, k * d:(k + 1) * d] for k in range(3))
    shift1, scale1, gate1 = (mod[1, :bn, k * d:(k + 1) * d] for k in range(3))
    shift_c = jnp.broadcast_to(mod[0, bn:bn + 1, 0:d], (bn, d))
    scale_c = jnp.broadcast_to(mod[0, bn:bn + 1, d:2 * d], (bn, d))

    w_in = ev_w_in[0].astype(BF16)
    xb_col = 3 * w_conv
    xb_ctx = _ctx_inproj(ctx, norm_g[0], shift_c, scale_c, w_in[:, xb_col:xb_col + w_lru], 16)

    def gate_params(k):
        wg = (0.5 * jnp.concatenate([ev_w_r[0, k], ev_w_i[0, k]], axis=-1)).astype(BF16)
        return dict(wg=wg, b_r=ev_b_r[0, k], b_i=ev_b_i[0, k], lam=ev_lam[0, k])

    sconv = dict(sconv_w=ev_sconv_w[0], sconv_b=ev_sconv_b[0])
    zero_state = jnp.zeros((bn, w_lru), F32)
    _, h_ctx_f = _lru_scan(xb_ctx, h0=zero_state, steps=16, reverse=False, name="ctx_scan0",
                           **sconv, **gate_params(0))
    _, h_ctx_b = _lru_scan(xb_ctx, h0=zero_state, steps=16, reverse=True, name="ctx_scan1",
                           **sconv, **gate_params(1))

    rows, u, sga, sgb, sc, hf, _ = _even_in(x, norm_g[0], shift0, scale0, w_in, h0=h_ctx_f,
                                            steps=16, **sconv, **gate_params(0))
    rows = _even_out(u, sga, sgb, hf, sc, rows, gate0, ev_conv_w[0], ev_conv_b[0],
                     ev_ln_g[0], ev_ln_b[0], ev_w_out[0].astype(BF16), h0=h_ctx_b, steps=16,
                     **gate_params(1))

    return _odd_layer(rows.reshape(seq, bn, d), norm_g[1], shift1, scale1, gate1,
                      od_w_in[0].astype(BF16), od_w_grp[0].astype(BF16), od_scale[0],
                      od_w_out[0].astype(BF16), final_g, nbb=16)
```

```python
import functools

import jax
import jax.numpy as jnp
from jax import lax
from jax.experimental import pallas as pl
from jax.experimental.pallas import tpu as pltpu

F32 = jnp.float32
BF16 = jnp.bfloat16

RMS_EPS = 1e-6
LN_EPS = 1e-5
LRU_C = 8.0
LOG2E = 1.4426950408889634
N_LRU_HEADS = 8
CONV_WIDTH = 31
CONV_HALF = 15
SHORT_CONV_WIDTH = 4
SHORT_CONV_LEAD = 2
TAPS_PER_FENCE = 6
CONV_ROW_TILE = 64
CHUNK_STEPS = 16
ODD_BATCH_ROWS = 16
MOD_TILE_N = 1024
GRID_W = 64
POOL_WINDOWS = (2, 4, 8, 16)

LANES = 128
SUBLANES = 8
VMEM_LIMIT_BYTES = 56 * 1024 * 1024


def _params(n_axes):
    return pltpu.CompilerParams(
        dimension_semantics=("arbitrary",) * n_axes,
        vmem_limit_bytes=VMEM_LIMIT_BYTES)


def _resident(shape, index_map):
    return pl.BlockSpec(shape, index_map, pipeline_mode=pl.Buffered(1))


def _sigmoid(x):
    return 0.5 * jnp.tanh(0.5 * x) + 0.5


def _silu(x):
    return x * _sigmoid(x)


def _modulated_rmsnorm(x, g, shift, scale, nb):
    rows, d = x.shape
    ms = jnp.mean(x * x, axis=-1, keepdims=True)
    y = x * lax.rsqrt(ms + RMS_EPS) * g
    y = y.reshape(rows // nb, nb, d) * (1.0 + scale)[None] + shift[None]
    return y.reshape(rows, d)


def _mod_kernel(c_ref, w_ref, b_ref, o_ref):
    s = _silu(c_ref[...])
    o_ref[0] = jnp.dot(s, w_ref[0], preferred_element_type=F32,
                       precision=lax.Precision.HIGHEST) + b_ref[0]


def _modulation(cc, mod_w, mod_b):
    depth, d, d3 = mod_w.shape
    r = cc.shape[0]
    tn = MOD_TILE_N
    return pl.pallas_call(
        _mod_kernel,
        grid=(depth, d3 // tn),
        in_specs=[pl.BlockSpec((r, d), lambda l, j: (0, 0)),
                  pl.BlockSpec((1, d, tn), lambda l, j: (l, 0, j)),
                  pl.BlockSpec((1, 1, tn), lambda l, j: (l, 0, j))],
        out_specs=pl.BlockSpec((1, r, tn), lambda l, j: (l, 0, j)),
        out_shape=jax.ShapeDtypeStruct((depth, r, d3), F32),
        compiler_params=_params(2),
        name="modulation",
    )(cc, mod_w, mod_b.reshape(depth, 1, d3))


def _lru_rate(lam):
    neg = -lam
    return LRU_C * (jnp.maximum(neg, 0.0) + jnp.log1p(jnp.exp(-jnp.abs(neg))))


def _short_conv(xe_s, r, nb, cw_ref, cb_ref, sl):
    u = cb_ref[:, sl] + cw_ref[0:1, sl] * xe_s[0:r, sl]
    for k in range(1, SHORT_CONV_WIDTH):
        u = u + cw_ref[k:k + 1, sl] * xe_s[k * nb:k * nb + r, sl]
    return u


def _lru_coeffs(conv_out, wg_ref, br_ref, bi_ref, lam_ref, a_s, b_s, between_heads=None):
    w = a_s.shape[-1]
    rate = _lru_rate(lam_ref[...])
    neg_rate_log2e = rate * (-LOG2E)
    half_br = 0.5 * br_ref[...]
    half_bi = 0.5 * bi_ref[...]
    hd = w // N_LRU_HEADS
    for h in range(N_LRU_HEADS):
        if between_heads is not None and h > 0:
            between_heads(h - 1)
        sl = slice(h * hd, (h + 1) * hd)
        u = conv_out(sl)
        g = jnp.dot(u.astype(BF16), wg_ref[h], preferred_element_type=F32)
        rg = 0.5 * jnp.tanh(g[:, :hd] + half_br[:, sl]) + 0.5
        ig = 0.5 * jnp.tanh(g[:, hd:] + half_bi[:, sl]) + 0.5
        p = rg * rate[:, sl]
        a = jnp.exp2(rg * neg_rate_log2e[:, sl])
        a_s[:, sl] = a
        y = jnp.tanh(p) * (a * a + 1.0)
        b_s[:, sl] = jnp.exp2(0.5 * jnp.log2(y)) * (ig * u)
    if between_heads is not None:
        between_heads(N_LRU_HEADS - 1)


def _lru_steps(a_s, b_s, h, hs_ref, steps, nb, reverse):
    def step(s, hcur):
        t = steps - 1 - s if reverse else s
        rows = pl.ds(pl.multiple_of(t * nb, nb), nb)
        hnew = a_s[rows, :] * hcur + b_s[rows, :]
        hs_ref[rows, :] = hnew
        return hnew

    return lax.fori_loop(0, steps, step, h)


def _lru_kernel(prev_ref, cur_ref, next_ref, cw_ref, cb_ref, wg_ref, br_ref, bi_ref,
                lam_ref, h0_ref, hs_ref, ht_ref, xe_s, a_s, b_s, h_s, *, nb, reverse):
    i = pl.program_id(0)
    nc = pl.num_programs(0)
    c = nc - 1 - i if reverse else i
    r, w = cur_ref.shape
    lead = SHORT_CONV_LEAD * nb

    @pl.when(i == 0)
    def _():
        h_s[...] = h0_ref[...]

    @pl.when(c > 0)
    def _():
        xe_s[0:lead] = prev_ref[...].astype(F32)

    @pl.when(c == 0)
    def _():
        xe_s[0:lead] = jnp.zeros((lead, w), F32)

    xe_s[lead:lead + r] = cur_ref[...].astype(F32)

    @pl.when(c < nc - 1)
    def _():
        xe_s[lead + r:lead + r + nb] = next_ref[...].astype(F32)

    @pl.when(c == nc - 1)
    def _():
        xe_s[lead + r:lead + r + nb] = jnp.zeros((nb, w), F32)

    _lru_coeffs(lambda sl: _short_conv(xe_s, r, nb, cw_ref, cb_ref, sl),
                wg_ref, br_ref, bi_ref, lam_ref, a_s, b_s)
    hfin = _lru_steps(a_s, b_s, h_s[...], hs_ref, r // nb, nb, reverse)
    h_s[...] = hfin
    ht_ref[...] = hfin


def _lru_scan(xb, sconv_w, sconv_b, wg, b_r, b_i, lam, h0, *, steps, reverse, name):
    n, w = xb.shape
    nb = h0.shape[0]
    r = steps * nb
    nc = n // r
    lead = SHORT_CONV_LEAD * nb
    chunk = (lambda i: nc - 1 - i) if reverse else (lambda i: i)
    const = lambda i: (0, 0)
    kern = functools.partial(_lru_kernel, nb=nb, reverse=reverse)
    return pl.pallas_call(
        kern,
        grid=(nc,),
        in_specs=[
            pl.BlockSpec((lead, w), lambda i: (jnp.maximum(chunk(i) * (r // lead) - 1, 0), 0)),
            pl.BlockSpec((r, w), lambda i: (chunk(i), 0)),
            pl.BlockSpec((nb, w), lambda i: (jnp.minimum((chunk(i) + 1) * (r // nb), n // nb - 1), 0)),
            pl.BlockSpec((SHORT_CONV_WIDTH, w), const),
            pl.BlockSpec((1, w), const),
            pl.BlockSpec(wg.shape, lambda i: (0, 0, 0)),
            pl.BlockSpec((1, w), const),
            pl.BlockSpec((1, w), const),
            pl.BlockSpec((1, w), const),
            pl.BlockSpec((nb, w), const),
        ],
        out_specs=[pl.BlockSpec((r, w), lambda i: (chunk(i), 0)),
                   pl.BlockSpec((nb, w), const)],
        out_shape=[jax.ShapeDtypeStruct((n, w), F32),
                   jax.ShapeDtypeStruct((nb, w), F32)],
        scratch_shapes=[pltpu.VMEM((lead + r + nb, w), F32),
                        pltpu.VMEM((r, w), F32),
                        pltpu.VMEM((r, w), F32),
                        pltpu.VMEM((nb, w), F32)],
        compiler_params=_params(1),
        name=name,
    )(xb, xb, xb, sconv_w, sconv_b.reshape(1, w), wg, b_r.reshape(1, w), b_i.reshape(1, w),
      lam.reshape(1, w), h0)


def _time_major(x_ref):
    nb, steps, d = x_ref.shape
    return jnp.swapaxes(x_ref[...], 0, 1).reshape(steps * nb, d)


def _ctx_inproj_kernel(x_ref, g_ref, shift_ref, scale_ref, w_ref, o_ref, *, nb):
    hx = _modulated_rmsnorm(_time_major(x_ref), g_ref[...], shift_ref[...], scale_ref[...],
                            nb).astype(BF16)
    o_ref[...] = jnp.dot(hx, w_ref[...], preferred_element_type=F32).astype(o_ref.dtype)


def _ctx_inproj(x, norm_g, shift, scale, w_bf16, steps):
    nb, seq, d = x.shape
    n = seq * nb
    tm = steps * nb
    width = w_bf16.shape[1]
    const = lambda i: (0, 0)
    return pl.pallas_call(
        functools.partial(_ctx_inproj_kernel, nb=nb),
        grid=(n // tm,),
        in_specs=[pl.BlockSpec((nb, steps, d), lambda i: (0, i, 0)),
                  pl.BlockSpec((1, d), const),
                  pl.BlockSpec((nb, d), const),
                  pl.BlockSpec((nb, d), const),
                  _resident(w_bf16.shape, const)],
        out_specs=pl.BlockSpec((tm, width), lambda i: (i, 0)),
        out_shape=jax.ShapeDtypeStruct((n, width), BF16),
        compiler_params=_params(1),
        name="ctx_inproj",
    )(x, norm_g.reshape(1, d), shift, scale, w_bf16)


def _even_in_kernel(x_ref, g_ref, shift_ref, scale_ref, w_ref, cw_ref, cb_ref, wg_ref, br_ref,
                    bi_ref, lam_ref, h0_ref, rows_ref, u_ref, sga_ref, sgb_ref, sc_ref, hs_ref,
                    ht_ref, xe_s, a_s, b_s, h_s, *, nb):
    c = pl.program_id(0)
    nc = pl.num_programs(0) - 1
    r, w = u_ref.shape
    lead = SHORT_CONV_LEAD * nb

    @pl.when(c == 0)
    def _():
        h_s[...] = h0_ref[...]
        xe_s[...] = jnp.zeros(xe_s.shape, F32)

    xe_s[0:lead + r] = xe_s[r:lead + 2 * r]

    x = _time_major(x_ref)
    rows_ref[...] = x
    hx = _modulated_rmsnorm(x, g_ref[...], shift_ref[...], scale_ref[...], nb).astype(BF16)

    def proj(group, cols=slice(0, w)):
        return jnp.dot(hx, w_ref[:, group * w + cols.start:group * w + cols.stop],
                       preferred_element_type=F32)

    xe_s[lead + r:lead + 2 * r] = proj(3) * (c < nc).astype(F32)

    q = w // (N_LRU_HEADS // 2)

    def proj_slice(i):
        if i < N_LRU_HEADS // 2:
            cols = slice(i * q, (i + 1) * q)
            u_ref[:, cols] = (proj(0, cols) * _sigmoid(proj(1, cols))).astype(u_ref.dtype)
        else:
            j = i - N_LRU_HEADS // 2
            o_ref, group = (sga_ref, 2) if j < N_LRU_HEADS // 4 else (sgb_ref, 4)
            cols = slice((j % (N_LRU_HEADS // 4)) * 2 * q, (j % (N_LRU_HEADS // 4) + 1) * 2 * q)
            o_ref[:, cols] = _silu(proj(group, cols)).astype(o_ref.dtype)

    def conv_out(sl):
        uc = _short_conv(xe_s, r, nb, cw_ref, cb_ref, sl)
        sc_ref[:, sl] = uc.astype(sc_ref.dtype)
        return uc

    _lru_coeffs(conv_out, wg_ref, br_ref, bi_ref, lam_ref, a_s, b_s, between_heads=proj_slice)

    h_prev = h_s[...]
    hfin = _lru_steps(a_s, b_s, h_prev, hs_ref, r // nb, nb, reverse=False)
    hfin = jnp.where(c > 0, hfin, h_prev)
    h_s[...] = hfin
    ht_ref[...] = hfin


def _even_in(x, norm_g, shift, scale, w_in_bf16, sconv_w, sconv_b, wg, b_r, b_i, lam, h0,
             *, steps):
    nb, seq, d = x.shape
    n = seq * nb
    w = sconv_w.shape[-1]
    r = steps * nb
    nc = n // r
    lead = SHORT_CONV_LEAD * nb
    assert w_in_bf16.shape == (d, 5 * w)
    const = lambda i: (0, 0)
    cur = lambda i: (jnp.minimum(i, nc - 1), 0)
    delayed = lambda i: (jnp.maximum(i - 1, 0), 0)
    kern = functools.partial(_even_in_kernel, nb=nb)
    act = jax.ShapeDtypeStruct((n, w), BF16)
    return pl.pallas_call(
        kern,
        grid=(nc + 1,),
        in_specs=[pl.BlockSpec((nb, steps, d), lambda i: (0, jnp.minimum(i, nc - 1), 0)),
                  pl.BlockSpec((1, d), const),
                  pl.BlockSpec((nb, d), const),
                  pl.BlockSpec((nb, d), const),
                  _resident(w_in_bf16.shape, const),
                  pl.BlockSpec((SHORT_CONV_WIDTH, w), const),
                  pl.BlockSpec((1, w), const),
                  _resident(wg.shape, lambda i: (0, 0, 0)),
                  pl.BlockSpec((1, w), const),
                  pl.BlockSpec((1, w), const),
                  pl.BlockSpec((1, w), const),
                  pl.BlockSpec((nb, w), const)],
        out_specs=[pl.BlockSpec((r, d), cur),
                   pl.BlockSpec((r, w), cur),
                   pl.BlockSpec((r, w), cur),
                   pl.BlockSpec((r, w), cur),
                   pl.BlockSpec((r, w), delayed),
                   pl.BlockSpec((r, w), delayed),
                   pl.BlockSpec((nb, w), const)],
        out_shape=[jax.ShapeDtypeStruct((n, d), F32), act, act, act, act,
                   jax.ShapeDtypeStruct((n, w), F32),
                   jax.ShapeDtypeStruct((nb, w), F32)],
        scratch_shapes=[pltpu.VMEM((lead + 2 * r, w), F32),
                        pltpu.VMEM((r, w), F32),
                        pltpu.VMEM((r, w), F32),
                        pltpu.VMEM((nb, w), F32)],
        compiler_params=_params(1),
        name="even_in",
    )(x, norm_g.reshape(1, d), shift, scale, w_in_bf16, sconv_w, sconv_b.reshape(1, w), wg,
      b_r.reshape(1, w), b_i.reshape(1, w), lam.reshape(1, w), h0)


def _even_out_kernel(uprev_ref, ucur_ref, unext_ref, sga_ref, sgb_ref, hf_ref, x_ref, gate_ref,
                     cw_ref, cb_ref, lg_ref, lb_ref, wo_ref, sc_ref, wg_ref, br_ref, bi_ref,
                     lam_ref, h0_ref, o_ref, ue_s, v_s, wrep_s, a_s, b_s, h_s, out_s,
                     *, nb, row_tile):
    i = pl.program_id(0)
    nc = pl.num_programs(0)
    c = nc - 1 - i
    r, w = ucur_ref.shape
    d = o_ref.shape[-1]
    halo = CONV_HALF * nb
    n_lane_tiles = w // LANES

    @pl.when(i == 0)
    def _():
        h_s[...] = h0_ref[...]
        for k in range(CONV_WIDTH):
            wrep_s[k] = jnp.broadcast_to(cw_ref[k:k + 1, :], (SUBLANES, w))

    def copy_in(dst_rows, src_ref):
        for lt in range(n_lane_tiles):
            ue_s[lt, dst_rows] = src_ref[:, lt * LANES:(lt + 1) * LANES].astype(F32)

    def fill(dst_rows, src_ref, valid):
        pl.when(valid)(lambda: copy_in(dst_rows, src_ref))

        @pl.when(jnp.logical_not(valid))
        def _():
            for lt in range(n_lane_tiles):
                ue_s[lt, dst_rows] = jnp.zeros((r, LANES), F32)

    fill(slice(0, r), uprev_ref, c > 0)
    copy_in(slice(r, 2 * r), ucur_ref)
    fill(slice(2 * r, 3 * r), unext_ref, c < nc - 1)

    n_row_tiles = r // row_tile
    sub = row_tile // SUBLANES
    for lt in range(n_lane_tiles):
        ls = slice(lt * LANES, (lt + 1) * LANES)
        bias = jnp.broadcast_to(cb_ref[:, ls], (SUBLANES, LANES))[None]

        def body(j, carry, lt=lt, ls=ls, bias=bias):
            base = pl.multiple_of(j * row_tile, row_tile) + (r - halo)

            def term(k):
                tap = ue_s[lt, pl.ds(base + k * nb, row_tile), :].reshape(sub, SUBLANES, LANES)
                return wrep_s[k, :, ls][None] * tap

            acc = jnp.broadcast_to(bias, (sub, SUBLANES, LANES))
            parked = None
            for k in range(CONV_WIDTH):
                acc = acc + term(k)
                if k % TAPS_PER_FENCE == TAPS_PER_FENCE - 1 and k < CONV_WIDTH - 1:
                    if parked is not None:
                        ue_s[lt, 3 * r:3 * r + row_tile, :] = parked.reshape(row_tile, LANES)
                    parked = acc
            v_s[lt, pl.ds(pl.multiple_of(j * row_tile, row_tile), row_tile), :] = (
                acc.reshape(row_tile, LANES))
            return carry

        lax.fori_loop(0, n_row_tiles, body, 0)

    v = jnp.concatenate([v_s[lt] for lt in range(n_lane_tiles)], axis=-1)
    mu = jnp.mean(v, axis=-1, keepdims=True)
    vc = v - mu
    var = jnp.mean(vc * vc, axis=-1, keepdims=True)
    yn = vc * lax.rsqrt(var + LN_EPS) * lg_ref[...] + lb_ref[...]
    ya = (_silu(yn) * sga_ref[...].astype(F32)).astype(BF16)

    q = d // (N_LRU_HEADS // 2)

    def out_slice(h):
        if h % 2 == 1:
            cols = slice((h // 2) * q, (h // 2 + 1) * q)
            out_s[:, cols] = jnp.dot(ya, wo_ref[0:w, cols], preferred_element_type=F32)

    _lru_coeffs(lambda sl: sc_ref[:, sl].astype(F32), wg_ref, br_ref, bi_ref, lam_ref, a_s, b_s,
                between_heads=out_slice)
    h_s[...] = _lru_steps(a_s, b_s, h_s[...], b_s, r // nb, nb, reverse=True)

    yb = ((hf_ref[...] + b_s[...]) * sgb_ref[...].astype(F32)).astype(BF16)
    out = out_s[...] + jnp.dot(yb, wo_ref[w:2 * w, :], preferred_element_type=F32)
    x = x_ref[...].reshape(r // nb, nb, d)
    o_ref[...] = (x + gate_ref[...][None] * out.reshape(r // nb, nb, d)).reshape(r, d)


def _even_out(u, sga, sgb, hf, sc, rows, gate, conv_w, conv_b, ln_g, ln_b, w_out_bf16,
              wg, b_r, b_i, lam, h0, *, steps):
    n, w = u.shape
    d = rows.shape[1]
    nb = gate.shape[0]
    r = steps * nb
    nc = n // r
    assert steps >= CONV_HALF
    const = lambda i: (0, 0)
    chunk = lambda i: nc - 1 - i
    tile = lambda i: (chunk(i), 0)
    row_tile = CONV_ROW_TILE
    kern = functools.partial(_even_out_kernel, nb=nb, row_tile=row_tile)
    return pl.pallas_call(
        kern,
        grid=(nc,),
        in_specs=[
            pl.BlockSpec((r, w), lambda i: (jnp.maximum(chunk(i) - 1, 0), 0)),
            pl.BlockSpec((r, w), tile),
            pl.BlockSpec((r, w), lambda i: (jnp.minimum(chunk(i) + 1, nc - 1), 0)),
            pl.BlockSpec((r, w), tile),
            pl.BlockSpec((r, w), tile),
            pl.BlockSpec((r, w), tile),
            pl.BlockSpec((r, d), tile),
            pl.BlockSpec((nb, d), const),
            pl.BlockSpec((CONV_WIDTH, w), const),
            pl.BlockSpec((1, w), const),
            pl.BlockSpec((1, w), const),
            pl.BlockSpec((1, w), const),
            _resident(w_out_bf16.shape, const),
            pl.BlockSpec((r, w), tile),
            _resident(wg.shape, lambda i: (0, 0, 0)),
            pl.BlockSpec((1, w), const),
            pl.BlockSpec((1, w), const),
            pl.BlockSpec((1, w), const),
            pl.BlockSpec((nb, w), const),
        ],
        out_specs=pl.BlockSpec((r, d), tile),
        out_shape=jax.ShapeDtypeStruct((n, d), F32),
        scratch_shapes=[pltpu.VMEM((w // LANES, 3 * r + row_tile, LANES), F32),
                        pltpu.VMEM((w // LANES, r, LANES), F32),
                        pltpu.VMEM((CONV_WIDTH, SUBLANES, w), F32),
                        pltpu.VMEM((r, w), F32),
                        pltpu.VMEM((r, w), F32),
                        pltpu.VMEM((nb, w), F32),
                        pltpu.VMEM((r, d), F32)],
        compiler_params=_params(1),
        name="even_out",
    )(u, u, u, sga, sgb, hf, rows, gate, conv_w, conv_b.reshape(1, w), ln_g.reshape(1, w),
      ln_b.reshape(1, w), w_out_bf16, sc, wg, b_r.reshape(1, w), b_i.reshape(1, w),
      lam.reshape(1, w), h0)


def _odd_kernel(x_ref, ng_ref, shift_ref, scale_ref, gate_ref, win_ref, wgrp_ref, pscale_ref,
                wo_ref, fg_ref, o_ref, ue_s, inv_s, *, nbb):
    first = (pl.program_id(0) == 0) & (pl.program_id(1) == 0)
    gw, _, d = x_ref.shape
    n_groups = len(POOL_WINDOWS)
    wp = pscale_ref.shape[-1]
    gd = wp // n_groups
    r = gw * nbb
    pad = (max(POOL_WINDOWS) // 2) * nbb

    @pl.when(first)
    def _():
        t = lax.broadcasted_iota(jnp.int32, (r, LANES), 0) // nbb
        for gi, win in enumerate(POOL_WINDOWS):
            half = win // 2
            cnt = jnp.minimum(t + half, gw) - jnp.maximum(t - half, 0)
            inv_s[gi] = 1.0 / cnt.astype(F32)
        zeros = jnp.zeros((pad, gd), F32)
        for buf in range(ue_s.shape[0]):
            ue_s[buf, 0:pad] = zeros
            ue_s[buf, pad + r:pad + r + pad] = zeros

    x = x_ref[...]
    hx = _modulated_rmsnorm(x.reshape(r, d), ng_ref[...], shift_ref[...], scale_ref[...],
                            nbb).astype(BF16)
    acc = None
    for gi, win in enumerate(POOL_WINDOWS):
        gs = slice(gi * gd, (gi + 1) * gd)
        buf = gi % ue_s.shape[0]
        ue_s[buf, pad:pad + r] = jnp.dot(hx, win_ref[:, gs], preferred_element_type=F32)
        sgate = _silu(jnp.dot(hx, win_ref[:, wp + gi * gd:wp + (gi + 1) * gd],
                              preferred_element_type=F32))
        e = ue_s[buf]
        cur = e[0:r + 2 * pad - nbb] + e[nbb:r + 2 * pad]
        lo = 1
        span = 2
        while span < win:
            sh = (span // 2) * nbb
            cur = cur[0:cur.shape[0] - 2 * sh] + cur[2 * sh:]
            lo += span // 2
            span *= 2
        start = pad - lo * nbb
        inv = jnp.concatenate([inv_s[gi]] * (gd // LANES), axis=-1)
        dlt = (cur[start:start + r] * inv - ue_s[buf, pad:pad + r]).astype(BF16)
        y = jnp.dot(dlt, wgrp_ref[gi], preferred_element_type=F32)
        y = y * pscale_ref[:, gs] * sgate
        part = jnp.dot(y.astype(BF16), wo_ref[gs, :], preferred_element_type=F32)
        acc = part if acc is None else acc + part

    xo = x + gate_ref[...][None] * acc.reshape(gw, nbb, d)
    ms = jnp.mean(xo * xo, axis=-1, keepdims=True)
    o_ref[...] = jnp.swapaxes(xo * lax.rsqrt(ms + RMS_EPS) * fg_ref[...][None], 0, 1)


def _odd_layer(rows3, norm_g, shift, scale, gate, w_in_bf16, w_grp_bf16, pscale, w_out_bf16,
               final_g, *, nbb):
    seq, nb, d = rows3.shape
    n_groups = len(POOL_WINDOWS)
    wp = pscale.shape[-1]
    gd = wp // n_groups
    pad = (max(POOL_WINDOWS) // 2) * nbb
    r = GRID_W * nbb
    blk = lambda i, j: (i, j, 0)
    per_batch = lambda i, j: (j, 0)
    const = lambda i, j: (0, 0)
    kern = functools.partial(_odd_kernel, nbb=nbb)
    return pl.pallas_call(
        kern,
        grid=(seq // GRID_W, nb // nbb),
        in_specs=[
            pl.BlockSpec((GRID_W, nbb, d), blk),
            pl.BlockSpec((1, d), const),
            pl.BlockSpec((nbb, d), per_batch),
            pl.BlockSpec((nbb, d), per_batch),
            pl.BlockSpec((nbb, d), per_batch),
            _resident(w_in_bf16.shape, const),
            _resident(w_grp_bf16.shape, lambda i, j: (0, 0, 0)),
            pl.BlockSpec((1, wp), const),
            _resident(w_out_bf16.shape, const),
            pl.BlockSpec((1, d), const),
        ],
        out_specs=pl.BlockSpec((nbb, GRID_W, d), lambda i, j: (j, i, 0)),
        out_shape=jax.ShapeDtypeStruct((nb, seq, d), F32),
        scratch_shapes=[pltpu.VMEM((2, r + 2 * pad, gd), F32),
                        pltpu.VMEM((n_groups, r, LANES), F32)],
        compiler_params=_params(2),
        name="odd_layer",
    )(rows3, norm_g.reshape(1, d), shift, scale, gate, w_in_bf16, w_grp_bf16,
      pscale.reshape(1, wp), w_out_bf16, final_g.reshape(1, d))


def kernel(x, c, ctx, c_ctx, norm_g, mod_w, mod_b, ev_w_in, ev_conv_w, ev_conv_b, ev_ln_g, ev_ln_b,
           ev_sconv_w, ev_sconv_b, ev_w_r, ev_b_r, ev_w_i, ev_b_i, ev_lam, ev_w_out,
           od_w_in, od_w_grp, od_scale, od_w_out, final_g):
    bn, seq, d = x.shape
    w_conv = ev_conv_w.shape[-1]
    w_lru = ev_sconv_w.shape[-1]
    assert w_conv == w_lru

    n_cond = -(-(bn + 1) // SUBLANES) * SUBLANES
    cc = jnp.zeros((n_cond, d), F32).at[:bn].set(c).at[bn].set(c_ctx)
    mod = _modulation(cc, mod_w, mod_b)
    shift0, scale0, gate0 = (mod[0, :bn, k * d:(k + 1) * d] for k in range(3))
    shift1, scale1, gate1 = (mod[1, :bn, k * d:(k + 1) * d] for k in range(3))
    shift_c = jnp.broadcast_to(mod[0, bn:bn + 1, 0:d], (bn, d))
    scale_c = jnp.broadcast_to(mod[0, bn:bn + 1, d:2 * d], (bn, d))

    w_in = ev_w_in[0].astype(BF16)
    xb_col = 3 * w_conv
    xb_ctx = _ctx_inproj(ctx, norm_g[0], shift_c, scale_c, w_in[:, xb_col:xb_col + w_lru],
                         CHUNK_STEPS)

    def gate_params(k):
        wg = (0.5 * jnp.concatenate([ev_w_r[0, k], ev_w_i[0, k]], axis=-1)).astype(BF16)
        return dict(wg=wg, b_r=ev_b_r[0, k], b_i=ev_b_i[0, k], lam=ev_lam[0, k])

    sconv = dict(sconv_w=ev_sconv_w[0], sconv_b=ev_sconv_b[0])
    zero_state = jnp.zeros((bn, w_lru), F32)
    _, h_ctx_f = _lru_scan(xb_ctx, h0=zero_state, steps=CHUNK_STEPS, reverse=False,
                           name="ctx_scan0", **sconv, **gate_params(0))
    _, h_ctx_b = _lru_scan(xb_ctx, h0=zero_state, steps=CHUNK_STEPS, reverse=True,
                           name="ctx_scan1", **sconv, **gate_params(1))

    rows, u, sga, sgb, sc, hf, _ = _even_in(x, norm_g[0], shift0, scale0, w_in, h0=h_ctx_f,
                                            steps=CHUNK_STEPS, **sconv, **gate_params(0))
    rows = _even_out(u, sga, sgb, hf, sc, rows, gate0, ev_conv_w[0], ev_conv_b[0],
                     ev_ln_g[0], ev_ln_b[0], ev_w_out[0].astype(BF16), h0=h_ctx_b,
                     steps=CHUNK_STEPS, **gate_params(1))

    return _odd_layer(rows.reshape(seq, bn, d), norm_g[1], shift1, scale1, gate1,
                      od_w_in[0].astype(BF16), od_w_grp[0].astype(BF16), od_scale[0],
                      od_w_out[0].astype(BF16), final_g, nbb=ODD_BATCH_ROWS)
```

```python
import functools

import jax
import jax.numpy as jnp
from jax import lax
from jax.experimental import pallas as pl
from jax.experimental.pallas import tpu as pltpu

F32 = jnp.float32
BF16 = jnp.bfloat16

RMS_EPS = 1e-6
LN_EPS = 1e-5
LRU_C = 8.0
LOG2E = 1.4426950408889634
N_LRU_HEADS = 8
CONV_WIDTH = 31
CONV_HALF = 15
SHORT_CONV_WIDTH = 4
SHORT_CONV_LEAD = 2
TAPS_PER_FENCE = 6
CONV_ROW_TILE = 64
CHUNK_STEPS = 16
ODD_BATCH_ROWS = 16
MOD_TILE_N = 1024
GRID_W = 64
POOL_WINDOWS = (2, 4, 8, 16)

LANES = 128
SUBLANES = 8
VMEM_LIMIT_BYTES = 56 * 1024 * 1024


def _params(n_axes):
    return pltpu.CompilerParams(
        dimension_semantics=("arbitrary",) * n_axes,
        vmem_limit_bytes=VMEM_LIMIT_BYTES)


def _resident(shape, index_map):
    return pl.BlockSpec(shape, index_map, pipeline_mode=pl.Buffered(1))


def _sigmoid(x):
    return 0.5 * jnp.tanh(0.5 * x) + 0.5


def _silu(x):
    return x * _sigmoid(x)


def _silu_of_twice(xh):
    return xh * (jnp.tanh(xh) + 1.0)


def _modulated_rmsnorm(x, g, shift, scale, nb):
    rows, d = x.shape
    ms = jnp.mean(x * x, axis=-1, keepdims=True)
    gain = g * (1.0 + scale)
    y = (x * lax.rsqrt(ms + RMS_EPS)).reshape(rows // nb, nb, d) * gain[None] + shift[None]
    return y.reshape(rows, d)


def _mod_kernel(c_ref, w_ref, b_ref, o_ref):
    s = _silu(c_ref[...])
    o_ref[0] = jnp.dot(s, w_ref[0], preferred_element_type=F32,
                       precision=lax.Precision.HIGHEST) + b_ref[0]


def _modulation(cc, mod_w, mod_b):
    depth, d, d3 = mod_w.shape
    r = cc.shape[0]
    tn = MOD_TILE_N
    return pl.pallas_call(
        _mod_kernel,
        grid=(depth, d3 // tn),
        in_specs=[pl.BlockSpec((r, d), lambda l, j: (0, 0)),
                  pl.BlockSpec((1, d, tn), lambda l, j: (l, 0, j)),
                  pl.BlockSpec((1, 1, tn), lambda l, j: (l, 0, j))],
        out_specs=pl.BlockSpec((1, r, tn), lambda l, j: (l, 0, j)),
        out_shape=jax.ShapeDtypeStruct((depth, r, d3), F32),
        compiler_params=_params(2),
        name="modulation",
    )(cc, mod_w, mod_b.reshape(depth, 1, d3))


def _lru_rate(lam):
    neg = -lam
    return LRU_C * (jnp.maximum(neg, 0.0) + jnp.log1p(jnp.exp(-jnp.abs(neg))))


def _short_conv(xe_s, r, nb, cw_ref, cb_ref, sl):
    u = cb_ref[:, sl] + cw_ref[0:1, sl] * xe_s[0:r, sl]
    for k in range(1, SHORT_CONV_WIDTH):
        u = u + cw_ref[k:k + 1, sl] * xe_s[k * nb:k * nb + r, sl]
    return u


def _lru_coeffs(conv_out, wg_ref, br_ref, bi_ref, lam_ref, a_s, b_s, between_heads=None):
    w = a_s.shape[-1]
    rate = _lru_rate(lam_ref[...])
    neg_rate_log2e = rate * (-LOG2E)
    half_br = 0.5 * br_ref[...]
    half_bi = 0.5 * bi_ref[...]
    hd = w // N_LRU_HEADS
    for h in range(N_LRU_HEADS):
        if between_heads is not None and h > 0:
            between_heads(h - 1)
        sl = slice(h * hd, (h + 1) * hd)
        u = conv_out(sl)
        g = jnp.dot(u.astype(BF16), wg_ref[h], preferred_element_type=F32)
        rg = 0.5 * jnp.tanh(g[:, :hd] + half_br[:, sl]) + 0.5
        ig = 0.5 * jnp.tanh(g[:, hd:] + half_bi[:, sl]) + 0.5
        p = rg * rate[:, sl]
        a = jnp.exp2(rg * neg_rate_log2e[:, sl])
        a_s[:, sl] = a
        y = jnp.tanh(p) * (a * a + 1.0)
        b_s[:, sl] = jnp.exp2(0.5 * jnp.log2(y)) * (ig * u)
    if between_heads is not None:
        between_heads(N_LRU_HEADS - 1)


def _lru_steps(a_s, b_s, h, hs_ref, steps, nb, reverse):
    def step(s, hcur):
        t = steps - 1 - s if reverse else s
        rows = pl.ds(pl.multiple_of(t * nb, nb), nb)
        hnew = a_s[rows, :] * hcur + b_s[rows, :]
        hs_ref[rows, :] = hnew
        return hnew

    return lax.fori_loop(0, steps, step, h)


def _lru_kernel(prev_ref, cur_ref, next_ref, cw_ref, cb_ref, wg_ref, br_ref, bi_ref,
                lam_ref, h0_ref, hs_ref, ht_ref, xe_s, a_s, b_s, h_s, *, nb, reverse):
    i = pl.program_id(0)
    nc = pl.num_programs(0)
    c = nc - 1 - i if reverse else i
    r, w = cur_ref.shape
    lead = SHORT_CONV_LEAD * nb

    @pl.when(i == 0)
    def _():
        h_s[...] = h0_ref[...]

    @pl.when(c > 0)
    def _():
        xe_s[0:lead] = prev_ref[...].astype(F32)

    @pl.when(c == 0)
    def _():
        xe_s[0:lead] = jnp.zeros((lead, w), F32)

    xe_s[lead:lead + r] = cur_ref[...].astype(F32)

    @pl.when(c < nc - 1)
    def _():
        xe_s[lead + r:lead + r + nb] = next_ref[...].astype(F32)

    @pl.when(c == nc - 1)
    def _():
        xe_s[lead + r:lead + r + nb] = jnp.zeros((nb, w), F32)

    _lru_coeffs(lambda sl: _short_conv(xe_s, r, nb, cw_ref, cb_ref, sl),
                wg_ref, br_ref, bi_ref, lam_ref, a_s, b_s)
    hfin = _lru_steps(a_s, b_s, h_s[...], hs_ref, r // nb, nb, reverse)
    h_s[...] = hfin
    ht_ref[...] = hfin


def _lru_scan(xb, sconv_w, sconv_b, wg, b_r, b_i, lam, h0, *, steps, reverse, name):
    n, w = xb.shape
    nb = h0.shape[0]
    r = steps * nb
    nc = n // r
    lead = SHORT_CONV_LEAD * nb
    chunk = (lambda i: nc - 1 - i) if reverse else (lambda i: i)
    const = lambda i: (0, 0)
    kern = functools.partial(_lru_kernel, nb=nb, reverse=reverse)
    return pl.pallas_call(
        kern,
        grid=(nc,),
        in_specs=[
            pl.BlockSpec((lead, w), lambda i: (jnp.maximum(chunk(i) * (r // lead) - 1, 0), 0)),
            pl.BlockSpec((r, w), lambda i: (chunk(i), 0)),
            pl.BlockSpec((nb, w), lambda i: (jnp.minimum((chunk(i) + 1) * (r // nb), n // nb - 1), 0)),
            pl.BlockSpec((SHORT_CONV_WIDTH, w), const),
            pl.BlockSpec((1, w), const),
            pl.BlockSpec(wg.shape, lambda i: (0, 0, 0)),
            pl.BlockSpec((1, w), const),
            pl.BlockSpec((1, w), const),
            pl.BlockSpec((1, w), const),
            pl.BlockSpec((nb, w), const),
        ],
        out_specs=[pl.BlockSpec((r, w), lambda i: (chunk(i), 0)),
                   pl.BlockSpec((nb, w), const)],
        out_shape=[jax.ShapeDtypeStruct((n, w), F32),
                   jax.ShapeDtypeStruct((nb, w), F32)],
        scratch_shapes=[pltpu.VMEM((lead + r + nb, w), F32),
                        pltpu.VMEM((r, w), F32),
                        pltpu.VMEM((r, w), F32),
                        pltpu.VMEM((nb, w), F32)],
        compiler_params=_params(1),
        name=name,
    )(xb, xb, xb, sconv_w, sconv_b.reshape(1, w), wg, b_r.reshape(1, w), b_i.reshape(1, w),
      lam.reshape(1, w), h0)


def _time_major(x_ref):
    nb, steps, d = x_ref.shape
    return jnp.swapaxes(x_ref[...], 0, 1).reshape(steps * nb, d)


def _ctx_inproj_kernel(x_ref, g_ref, shift_ref, scale_ref, w_ref, o_ref, *, nb):
    hx = _modulated_rmsnorm(_time_major(x_ref), g_ref[...], shift_ref[...], scale_ref[...],
                            nb).astype(BF16)
    o_ref[...] = jnp.dot(hx, w_ref[...], preferred_element_type=F32).astype(o_ref.dtype)


def _ctx_inproj(x, norm_g, shift, scale, w_bf16, steps):
    nb, seq, d = x.shape
    n = seq * nb
    tm = steps * nb
    width = w_bf16.shape[1]
    const = lambda i: (0, 0)
    return pl.pallas_call(
        functools.partial(_ctx_inproj_kernel, nb=nb),
        grid=(n // tm,),
        in_specs=[pl.BlockSpec((nb, steps, d), lambda i: (0, i, 0)),
                  pl.BlockSpec((1, d), const),
                  pl.BlockSpec((nb, d), const),
                  pl.BlockSpec((nb, d), const),
                  _resident(w_bf16.shape, const)],
        out_specs=pl.BlockSpec((tm, width), lambda i: (i, 0)),
        out_shape=jax.ShapeDtypeStruct((n, width), BF16),
        compiler_params=_params(1),
        name="ctx_inproj",
    )(x, norm_g.reshape(1, d), shift, scale, w_bf16)


def _even_in_kernel(x_ref, g_ref, shift_ref, scale_ref, w_ref, cw_ref, cb_ref, wg_ref, br_ref,
                    bi_ref, lam_ref, h0_ref, rows_ref, u_ref, sga_ref, sgb_ref, sc_ref, hs_ref,
                    ht_ref, xe_s, a_s, b_s, h_s, *, nb):
    c = pl.program_id(0)
    nc = pl.num_programs(0) - 1
    r, w = u_ref.shape
    lead = SHORT_CONV_LEAD * nb

    @pl.when(c == 0)
    def _():
        h_s[...] = h0_ref[...]
        xe_s[...] = jnp.zeros(xe_s.shape, F32)

    xe_s[0:lead + r] = xe_s[r:lead + 2 * r]

    x = _time_major(x_ref)
    rows_ref[...] = x
    hx = _modulated_rmsnorm(x, g_ref[...], shift_ref[...], scale_ref[...], nb).astype(BF16)

    def proj(group, cols=slice(0, w)):
        return jnp.dot(hx, w_ref[:, group * w + cols.start:group * w + cols.stop],
                       preferred_element_type=F32)

    xe_s[lead + r:lead + 2 * r] = proj(3) * (c < nc).astype(F32)

    q = w // (N_LRU_HEADS // 2)

    def proj_slice(i):
        if i < N_LRU_HEADS // 2:
            cols = slice(i * q, (i + 1) * q)
            u_ref[:, cols] = (proj(0, cols) * (jnp.tanh(proj(1, cols)) + 1.0)).astype(u_ref.dtype)
        else:
            j = i - N_LRU_HEADS // 2
            o_ref, group = (sga_ref, 2) if j < N_LRU_HEADS // 4 else (sgb_ref, 4)
            cols = slice((j % (N_LRU_HEADS // 4)) * 2 * q, (j % (N_LRU_HEADS // 4) + 1) * 2 * q)
            o_ref[:, cols] = _silu_of_twice(proj(group, cols)).astype(o_ref.dtype)

    def conv_out(sl):
        uc = _short_conv(xe_s, r, nb, cw_ref, cb_ref, sl)
        sc_ref[:, sl] = uc.astype(sc_ref.dtype)
        return uc

    _lru_coeffs(conv_out, wg_ref, br_ref, bi_ref, lam_ref, a_s, b_s, between_heads=proj_slice)

    h_prev = h_s[...]
    hfin = _lru_steps(a_s, b_s, h_prev, hs_ref, r // nb, nb, reverse=False)
    hfin = jnp.where(c > 0, hfin, h_prev)
    h_s[...] = hfin
    ht_ref[...] = hfin


def _even_in(x, norm_g, shift, scale, w_in_bf16, sconv_w, sconv_b, wg, b_r, b_i, lam, h0,
             *, steps):
    nb, seq, d = x.shape
    n = seq * nb
    w = sconv_w.shape[-1]
    r = steps * nb
    nc = n // r
    lead = SHORT_CONV_LEAD * nb
    assert w_in_bf16.shape == (d, 5 * w)
    const = lambda i: (0, 0)
    cur = lambda i: (jnp.minimum(i, nc - 1), 0)
    delayed = lambda i: (jnp.maximum(i - 1, 0), 0)
    kern = functools.partial(_even_in_kernel, nb=nb)
    act = jax.ShapeDtypeStruct((n, w), BF16)
    return pl.pallas_call(
        kern,
        grid=(nc + 1,),
        in_specs=[pl.BlockSpec((nb, steps, d), lambda i: (0, jnp.minimum(i, nc - 1), 0)),
                  pl.BlockSpec((1, d), const),
                  pl.BlockSpec((nb, d), const),
                  pl.BlockSpec((nb, d), const),
                  _resident(w_in_bf16.shape, const),
                  pl.BlockSpec((SHORT_CONV_WIDTH, w), const),
                  pl.BlockSpec((1, w), const),
                  _resident(wg.shape, lambda i: (0, 0, 0)),
                  pl.BlockSpec((1, w), const),
                  pl.BlockSpec((1, w), const),
                  pl.BlockSpec((1, w), const),
                  pl.BlockSpec((nb, w), const)],
        out_specs=[pl.BlockSpec((r, d), cur),
                   pl.BlockSpec((r, w), cur),
                   pl.BlockSpec((r, w), cur),
                   pl.BlockSpec((r, w), cur),
                   pl.BlockSpec((r, w), delayed),
                   pl.BlockSpec((r, w), delayed),
                   pl.BlockSpec((nb, w), const)],
        out_shape=[jax.ShapeDtypeStruct((n, d), F32), act, act, act, act,
                   jax.ShapeDtypeStruct((n, w), F32),
                   jax.ShapeDtypeStruct((nb, w), F32)],
        scratch_shapes=[pltpu.VMEM((lead + 2 * r, w), F32),
                        pltpu.VMEM((r, w), F32),
                        pltpu.VMEM((r, w), F32),
                        pltpu.VMEM((nb, w), F32)],
        compiler_params=_params(1),
        name="even_in",
    )(x, norm_g.reshape(1, d), shift, scale, w_in_bf16, sconv_w, sconv_b.reshape(1, w), wg,
      b_r.reshape(1, w), b_i.reshape(1, w), lam.reshape(1, w), h0)


def _even_out_kernel(uprev_ref, ucur_ref, unext_ref, sga_ref, sgb_ref, hf_ref, x_ref, gate_ref,
                     cw_ref, cb_ref, lg_ref, lb_ref, wo_ref, sc_ref, wg_ref, br_ref, bi_ref,
                     lam_ref, h0_ref, o_ref, ue_s, v_s, wrep_s, a_s, b_s, h_s, out_s,
                     *, nb, row_tile):
    i = pl.program_id(0)
    nc = pl.num_programs(0)
    c = nc - 1 - i
    r, w = ucur_ref.shape
    d = o_ref.shape[-1]
    halo = CONV_HALF * nb
    n_lane_tiles = w // LANES

    @pl.when(i == 0)
    def _():
        h_s[...] = h0_ref[...]
        for k in range(CONV_WIDTH):
            wrep_s[k] = jnp.broadcast_to(cw_ref[k:k + 1, :], (SUBLANES, w))

    def copy_in(dst_rows, src_ref):
        for lt in range(n_lane_tiles):
            ue_s[lt, dst_rows] = src_ref[:, lt * LANES:(lt + 1) * LANES].astype(F32)

    def fill(dst_rows, src_ref, valid):
        pl.when(valid)(lambda: copy_in(dst_rows, src_ref))

        @pl.when(jnp.logical_not(valid))
        def _():
            for lt in range(n_lane_tiles):
                ue_s[lt, dst_rows] = jnp.zeros((r, LANES), F32)

    fill(slice(0, r), uprev_ref, c > 0)
    copy_in(slice(r, 2 * r), ucur_ref)
    fill(slice(2 * r, 3 * r), unext_ref, c < nc - 1)

    n_row_tiles = r // row_tile
    sub = row_tile // SUBLANES
    for lt in range(n_lane_tiles):
        ls = slice(lt * LANES, (lt + 1) * LANES)
        bias = jnp.broadcast_to(cb_ref[:, ls], (SUBLANES, LANES))[None]

        def body(j, carry, lt=lt, ls=ls, bias=bias):
            base = pl.multiple_of(j * row_tile, row_tile) + (r - halo)

            def term(k):
                tap = ue_s[lt, pl.ds(base + k * nb, row_tile), :].reshape(sub, SUBLANES, LANES)
                return wrep_s[k, :, ls][None] * tap

            acc = jnp.broadcast_to(bias, (sub, SUBLANES, LANES))
            parked = None
            for k in range(CONV_WIDTH):
                acc = acc + term(k)
                if k % TAPS_PER_FENCE == TAPS_PER_FENCE - 1 and k < CONV_WIDTH - 1:
                    if parked is not None:
                        ue_s[lt, 3 * r:3 * r + row_tile, :] = parked.reshape(row_tile, LANES)
                    parked = acc
            v_s[lt, pl.ds(pl.multiple_of(j * row_tile, row_tile), row_tile), :] = (
                acc.reshape(row_tile, LANES))
            return carry

        lax.fori_loop(0, n_row_tiles, body, 0)

    v = jnp.concatenate([v_s[lt] for lt in range(n_lane_tiles)], axis=-1)
    mu = jnp.mean(v, axis=-1, keepdims=True)
    vc = v - mu
    var = jnp.mean(vc * vc, axis=-1, keepdims=True)
    yh = vc * lax.rsqrt(var + LN_EPS) * (0.5 * lg_ref[...]) + 0.5 * lb_ref[...]
    ya = (_silu_of_twice(yh) * sga_ref[...].astype(F32)).astype(BF16)

    q = d // (N_LRU_HEADS // 2)

    def out_slice(h):
        if h % 2 == 1:
            cols = slice((h // 2) * q, (h // 2 + 1) * q)
            out_s[:, cols] = jnp.dot(ya, wo_ref[0:w, cols], preferred_element_type=F32)

    _lru_coeffs(lambda sl: sc_ref[:, sl].astype(F32), wg_ref, br_ref, bi_ref, lam_ref, a_s, b_s,
                between_heads=out_slice)
    h_s[...] = _lru_steps(a_s, b_s, h_s[...], b_s, r // nb, nb, reverse=True)

    yb = ((hf_ref[...] + b_s[...]) * sgb_ref[...].astype(F32)).astype(BF16)
    out = out_s[...] + jnp.dot(yb, wo_ref[w:2 * w, :], preferred_element_type=F32)
    x = x_ref[...].reshape(r // nb, nb, d)
    o_ref[...] = (x + gate_ref[...][None] * out.reshape(r // nb, nb, d)).reshape(r, d)


def _even_out(u, sga, sgb, hf, sc, rows, gate, conv_w, conv_b, ln_g, ln_b, w_out_bf16,
              wg, b_r, b_i, lam, h0, *, steps):
    n, w = u.shape
    d = rows.shape[1]
    nb = gate.shape[0]
    r = steps * nb
    nc = n // r
    assert steps >= CONV_HALF
    const = lambda i: (0, 0)
    chunk = lambda i: nc - 1 - i
    tile = lambda i: (chunk(i), 0)
    row_tile = CONV_ROW_TILE
    kern = functools.partial(_even_out_kernel, nb=nb, row_tile=row_tile)
    return pl.pallas_call(
        kern,
        grid=(nc,),
        in_specs=[
            pl.BlockSpec((r, w), lambda i: (jnp.maximum(chunk(i) - 1, 0), 0)),
            pl.BlockSpec((r, w), tile),
            pl.BlockSpec((r, w), lambda i: (jnp.minimum(chunk(i) + 1, nc - 1), 0)),
            pl.BlockSpec((r, w), tile),
            pl.BlockSpec((r, w), tile),
            pl.BlockSpec((r, w), tile),
            pl.BlockSpec((r, d), tile),
            pl.BlockSpec((nb, d), const),
            pl.BlockSpec((CONV_WIDTH, w), const),
            pl.BlockSpec((1, w), const),
            pl.BlockSpec((1, w), const),
            pl.BlockSpec((1, w), const),
            _resident(w_out_bf16.shape, const),
            pl.BlockSpec((r, w), tile),
            _resident(wg.shape, lambda i: (0, 0, 0)),
            pl.BlockSpec((1, w), const),
            pl.BlockSpec((1, w), const),
            pl.BlockSpec((1, w), const),
            pl.BlockSpec((nb, w), const),
        ],
        out_specs=pl.BlockSpec((r, d), tile),
        out_shape=jax.ShapeDtypeStruct((n, d), F32),
        scratch_shapes=[pltpu.VMEM((w // LANES, 3 * r + row_tile, LANES), F32),
                        pltpu.VMEM((w // LANES, r, LANES), F32),
                        pltpu.VMEM((CONV_WIDTH, SUBLANES, w), F32),
                        pltpu.VMEM((r, w), F32),
                        pltpu.VMEM((r, w), F32),
                        pltpu.VMEM((nb, w), F32),
                        pltpu.VMEM((r, d), F32)],
        compiler_params=_params(1),
        name="even_out",
    )(u, u, u, sga, sgb, hf, rows, gate, conv_w, conv_b.reshape(1, w), ln_g.reshape(1, w),
      ln_b.reshape(1, w), w_out_bf16, sc, wg, b_r.reshape(1, w), b_i.reshape(1, w),
      lam.reshape(1, w), h0)


def _odd_kernel(x_ref, ng_ref, shift_ref, scale_ref, gate_ref, win_ref, wgrp_ref, pscale_ref,
                wo_ref, fg_ref, o_ref, ue_s, inv_s, *, nbb):
    first = (pl.program_id(0) == 0) & (pl.program_id(1) == 0)
    gw, _, d = x_ref.shape
    n_groups = len(POOL_WINDOWS)
    wp = pscale_ref.shape[-1]
    gd = wp // n_groups
    r = gw * nbb
    pad = (max(POOL_WINDOWS) // 2) * nbb

    @pl.when(first)
    def _():
        t = lax.broadcasted_iota(jnp.int32, (r, LANES), 0) // nbb
        for gi, win in enumerate(POOL_WINDOWS):
            half = win // 2
            cnt = jnp.minimum(t + half, gw) - jnp.maximum(t - half, 0)
            inv_s[gi] = 1.0 / cnt.astype(F32)
        zeros = jnp.zeros((pad, gd), F32)
        for buf in range(ue_s.shape[0]):
            ue_s[buf, 0:pad] = zeros
            ue_s[buf, pad + r:pad + r + pad] = zeros

    x = x_ref[...]
    hx = _modulated_rmsnorm(x.reshape(r, d), ng_ref[...], shift_ref[...], scale_ref[...],
                            nbb).astype(BF16)
    acc = None
    for gi, win in enumerate(POOL_WINDOWS):
        gs = slice(gi * gd, (gi + 1) * gd)
        buf = gi % ue_s.shape[0]
        ue_s[buf, pad:pad + r] = jnp.dot(hx, win_ref[:, gs], preferred_element_type=F32)
        sgate = _silu_of_twice(jnp.dot(hx, win_ref[:, wp + gi * gd:wp + (gi + 1) * gd],
                                       preferred_element_type=F32))
        e = ue_s[buf]
        cur = e[0:r + 2 * pad - nbb] + e[nbb:r + 2 * pad]
        lo = 1
        span = 2
        while span < win:
            sh = (span // 2) * nbb
            cur = cur[0:cur.shape[0] - 2 * sh] + cur[2 * sh:]
            lo += span // 2
            span *= 2
        start = pad - lo * nbb
        inv = jnp.concatenate([inv_s[gi]] * (gd // LANES), axis=-1)
        dlt = (cur[start:start + r] * inv - ue_s[buf, pad:pad + r]).astype(BF16)
        y = jnp.dot(dlt, wgrp_ref[gi], preferred_element_type=F32)
        y = y * pscale_ref[:, gs] * sgate
        part = jnp.dot(y.astype(BF16), wo_ref[gs, :], preferred_element_type=F32)
        acc = part if acc is None else acc + part

    xo = x + gate_ref[...][None] * acc.reshape(gw, nbb, d)
    ms = jnp.mean(xo * xo, axis=-1, keepdims=True)
    o_ref[...] = jnp.swapaxes(xo * lax.rsqrt(ms + RMS_EPS) * fg_ref[...][None], 0, 1)


def _odd_layer(rows3, norm_g, shift, scale, gate, w_in_bf16, w_grp_bf16, pscale, w_out_bf16,
               final_g, *, nbb):
    seq, nb, d = rows3.shape
    n_groups = len(POOL_WINDOWS)
    wp = pscale.shape[-1]
    gd = wp // n_groups
    pad = (max(POOL_WINDOWS) // 2) * nbb
    r = GRID_W * nbb
    blk = lambda i, j: (i, j, 0)
    per_batch = lambda i, j: (j, 0)
    const = lambda i, j: (0, 0)
    kern = functools.partial(_odd_kernel, nbb=nbb)
    return pl.pallas_call(
        kern,
        grid=(seq // GRID_W, nb // nbb),
        in_specs=[
            pl.BlockSpec((GRID_W, nbb, d), blk),
            pl.BlockSpec((1, d), const),
            pl.BlockSpec((nbb, d), per_batch),
            pl.BlockSpec((nbb, d), per_batch),
            pl.BlockSpec((nbb, d), per_batch),
            _resident(w_in_bf16.shape, const),
            _resident(w_grp_bf16.shape, lambda i, j: (0, 0, 0)),
            pl.BlockSpec((1, wp), const),
            _resident(w_out_bf16.shape, const),
            pl.BlockSpec((1, d), const),
        ],
        out_specs=pl.BlockSpec((nbb, GRID_W, d), lambda i, j: (j, i, 0)),
        out_shape=jax.ShapeDtypeStruct((nb, seq, d), F32),
        scratch_shapes=[pltpu.VMEM((2, r + 2 * pad, gd), F32),
                        pltpu.VMEM((n_groups, r, LANES), F32)],
        compiler_params=_params(2),
        name="odd_layer",
    )(rows3, norm_g.reshape(1, d), shift, scale, gate, w_in_bf16, w_grp_bf16,
      pscale.reshape(1, wp), w_out_bf16, final_g.reshape(1, d))


def kernel(x, c, ctx, c_ctx, norm_g, mod_w, mod_b, ev_w_in, ev_conv_w, ev_conv_b, ev_ln_g, ev_ln_b,
           ev_sconv_w, ev_sconv_b, ev_w_r, ev_b_r, ev_w_i, ev_b_i, ev_lam, ev_w_out,
           od_w_in, od_w_grp, od_scale, od_w_out, final_g):
    bn, seq, d = x.shape
    w_conv = ev_conv_w.shape[-1]
    w_lru = ev_sconv_w.shape[-1]
    assert w_conv == w_lru

    n_cond = -(-(bn + 1) // SUBLANES) * SUBLANES
    cc = jnp.zeros((n_cond, d), F32).at[:bn].set(c).at[bn].set(c_ctx)
    mod = _modulation(cc, mod_w, mod_b)
    shift0, scale0, gate0 = (mod[0, :bn, k * d:(k + 1) * d] for k in range(3))
    shift1, scale1, gate1 = (mod[1, :bn, k * d:(k + 1) * d] for k in range(3))
    shift_c = jnp.broadcast_to(mod[0, bn:bn + 1, 0:d], (bn, d))
    scale_c = jnp.broadcast_to(mod[0, bn:bn + 1, d:2 * d], (bn, d))

    xb_col = 3 * w_conv
    halved = jnp.full((ev_w_in.shape[-1],), 0.5, F32).at[xb_col:xb_col + w_lru].set(1.0)
    w_in = (ev_w_in[0] * halved).astype(BF16)
    xb_ctx = _ctx_inproj(ctx, norm_g[0], shift_c, scale_c, w_in[:, xb_col:xb_col + w_lru],
                         CHUNK_STEPS)

    def gate_params(k):
        wg = (0.5 * jnp.concatenate([ev_w_r[0, k], ev_w_i[0, k]], axis=-1)).astype(BF16)
        return dict(wg=wg, b_r=ev_b_r[0, k], b_i=ev_b_i[0, k], lam=ev_lam[0, k])

    sconv = dict(sconv_w=ev_sconv_w[0], sconv_b=ev_sconv_b[0])
    zero_state = jnp.zeros((bn, w_lru), F32)
    _, h_ctx_f = _lru_scan(xb_ctx, h0=zero_state, steps=CHUNK_STEPS, reverse=False,
                           name="ctx_scan0", **sconv, **gate_params(0))
    _, h_ctx_b = _lru_scan(xb_ctx, h0=zero_state, steps=CHUNK_STEPS, reverse=True,
                           name="ctx_scan1", **sconv, **gate_params(1))

    rows, u, sga, sgb, sc, hf, _ = _even_in(x, norm_g[0], shift0, scale0, w_in, h0=h_ctx_f,
                                            steps=CHUNK_STEPS, **sconv, **gate_params(0))
    rows = _even_out(u, sga, sgb, hf, sc, rows, gate0, ev_conv_w[0], ev_conv_b[0],
                     ev_ln_g[0], ev_ln_b[0], ev_w_out[0].astype(BF16), h0=h_ctx_b,
                     steps=CHUNK_STEPS, **gate_params(1))

    w_pool = od_scale.shape[-1]
    od_halved = jnp.ones((od_w_in.shape[-1],), F32).at[w_pool:].set(0.5)
    return _odd_layer(rows.reshape(seq, bn, d), norm_g[1], shift1, scale1, gate1,
                      (od_w_in[0] * od_halved).astype(BF16), od_w_grp[0].astype(BF16), od_scale[0],
                      od_w_out[0].astype(BF16), final_g, nbb=ODD_BATCH_ROWS)
```

```python
import functools

import jax
import jax.numpy as jnp
from jax import lax
from jax.experimental import pallas as pl
from jax.experimental.pallas import tpu as pltpu

F32 = jnp.float32
BF16 = jnp.bfloat16

RMS_EPS = 1e-6
LN_EPS = 1e-5
LRU_C = 8.0
LOG2E = 1.4426950408889634
N_LRU_HEADS = 8
CONV_WIDTH = 31
CONV_HALF = 15
SHORT_CONV_WIDTH = 4
SHORT_CONV_LEAD = 2
TAPS_PER_FENCE = 6
CONV_ROW_TILE = 64
CHUNK_STEPS = 16
ODD_BATCH_ROWS = 16
MOD_TILE_N = 1024
GRID_W = 64
POOL_WINDOWS = (2, 4, 8, 16)

LANES = 128
SUBLANES = 8
VMEM_LIMIT_BYTES = 56 * 1024 * 1024


def _params(n_axes):
    return pltpu.CompilerParams(
        dimension_semantics=("arbitrary",) * n_axes,
        vmem_limit_bytes=VMEM_LIMIT_BYTES)


def _resident(shape, index_map):
    return pl.BlockSpec(shape, index_map, pipeline_mode=pl.Buffered(1))


def _sigmoid(x):
    return 0.5 * jnp.tanh(0.5 * x) + 0.5


def _silu(x):
    return x * _sigmoid(x)


def _silu_of_twice(xh):
    return xh * (jnp.tanh(xh) + 1.0)


def _modulated_rmsnorm(x, g, shift, scale, nb):
    rows, d = x.shape
    ms = jnp.mean(x * x, axis=-1, keepdims=True)
    gain = g * (1.0 + scale)
    y = (x * lax.rsqrt(ms + RMS_EPS)).reshape(rows // nb, nb, d) * gain[None] + shift[None]
    return y.reshape(rows, d)


def _mod_kernel(c_ref, w_ref, b_ref, o_ref):
    s = _silu(c_ref[...])
    o_ref[0] = jnp.dot(s, w_ref[0], preferred_element_type=F32,
                       precision=lax.Precision.HIGHEST) + b_ref[0]


def _modulation(cc, mod_w, mod_b):
    depth, d, d3 = mod_w.shape
    r = cc.shape[0]
    tn = MOD_TILE_N
    return pl.pallas_call(
        _mod_kernel,
        grid=(depth, d3 // tn),
        in_specs=[pl.BlockSpec((r, d), lambda l, j: (0, 0)),
                  pl.BlockSpec((1, d, tn), lambda l, j: (l, 0, j)),
                  pl.BlockSpec((1, 1, tn), lambda l, j: (l, 0, j))],
        out_specs=pl.BlockSpec((1, r, tn), lambda l, j: (l, 0, j)),
        out_shape=jax.ShapeDtypeStruct((depth, r, d3), F32),
        compiler_params=_params(2),
        name="modulation",
    )(cc, mod_w, mod_b.reshape(depth, 1, d3))


def _lru_rate(lam):
    neg = -lam
    return LRU_C * (jnp.maximum(neg, 0.0) + jnp.log1p(jnp.exp(-jnp.abs(neg))))


def _short_conv(xe_s, r, nb, cw_ref, cb_ref, sl):
    u = cb_ref[:, sl] + cw_ref[0:1, sl] * xe_s[0:r, sl]
    for k in range(1, SHORT_CONV_WIDTH):
        u = u + cw_ref[k:k + 1, sl] * xe_s[k * nb:k * nb + r, sl]
    return u


def _lru_coeffs(conv_out, wg_ref, br_ref, bi_ref, lam_ref, a_s, b_s, between_heads=None):
    w = a_s.shape[-1]
    rate = _lru_rate(lam_ref[...])
    neg_rate_log2e = rate * (-LOG2E)
    half_br = 0.5 * br_ref[...]
    half_bi = 0.5 * bi_ref[...]
    hd = w // N_LRU_HEADS
    for h in range(N_LRU_HEADS):
        if between_heads is not None and h > 0:
            between_heads(h - 1)
        sl = slice(h * hd, (h + 1) * hd)
        u = conv_out(sl)
        g = jnp.dot(u.astype(BF16), wg_ref[h], preferred_element_type=F32)
        rg = 0.5 * jnp.tanh(g[:, :hd] + half_br[:, sl]) + 0.5
        ig = 0.5 * jnp.tanh(g[:, hd:] + half_bi[:, sl]) + 0.5
        p = rg * rate[:, sl]
        a = jnp.exp2(rg * neg_rate_log2e[:, sl])
        a_s[:, sl] = a
        y = jnp.tanh(p) * (a * a + 1.0)
        b_s[:, sl] = jnp.exp2(0.5 * jnp.log2(y)) * (ig * u)
    if between_heads is not None:
        between_heads(N_LRU_HEADS - 1)


def _lru_steps(a_s, b_s, h, hs_ref, steps, nb, reverse):
    def step(s, hcur):
        t = steps - 1 - s if reverse else s
        rows = pl.ds(pl.multiple_of(t * nb, nb), nb)
        hnew = a_s[rows, :] * hcur + b_s[rows, :]
        hs_ref[rows, :] = hnew
        return hnew

    return lax.fori_loop(0, steps, step, h, unroll=True)


def _lru_kernel(prev_ref, cur_ref, next_ref, cw_ref, cb_ref, wg_ref, br_ref, bi_ref,
                lam_ref, h0_ref, hs_ref, ht_ref, xe_s, a_s, b_s, h_s, *, nb, reverse):
    i = pl.program_id(0)
    nc = pl.num_programs(0)
    c = nc - 1 - i if reverse else i
    r, w = cur_ref.shape
    lead = SHORT_CONV_LEAD * nb

    @pl.when(i == 0)
    def _():
        h_s[...] = h0_ref[...]

    @pl.when(c > 0)
    def _():
        xe_s[0:lead] = prev_ref[...].astype(F32)

    @pl.when(c == 0)
    def _():
        xe_s[0:lead] = jnp.zeros((lead, w), F32)

    xe_s[lead:lead + r] = cur_ref[...].astype(F32)

    @pl.when(c < nc - 1)
    def _():
        xe_s[lead + r:lead + r + nb] = next_ref[...].astype(F32)

    @pl.when(c == nc - 1)
    def _():
        xe_s[lead + r:lead + r + nb] = jnp.zeros((nb, w), F32)

    _lru_coeffs(lambda sl: _short_conv(xe_s, r, nb, cw_ref, cb_ref, sl),
                wg_ref, br_ref, bi_ref, lam_ref, a_s, b_s)
    hfin = _lru_steps(a_s, b_s, h_s[...], hs_ref, r // nb, nb, reverse)
    h_s[...] = hfin
    ht_ref[...] = hfin


def _lru_scan(xb, sconv_w, sconv_b, wg, b_r, b_i, lam, h0, *, steps, reverse, name):
    n, w = xb.shape
    nb = h0.shape[0]
    r = steps * nb
    nc = n // r
    lead = SHORT_CONV_LEAD * nb
    chunk = (lambda i: nc - 1 - i) if reverse else (lambda i: i)
    const = lambda i: (0, 0)
    kern = functools.partial(_lru_kernel, nb=nb, reverse=reverse)
    return pl.pallas_call(
        kern,
        grid=(nc,),
        in_specs=[
            pl.BlockSpec((lead, w), lambda i: (jnp.maximum(chunk(i) * (r // lead) - 1, 0), 0)),
            pl.BlockSpec((r, w), lambda i: (chunk(i), 0)),
            pl.BlockSpec((nb, w), lambda i: (jnp.minimum((chunk(i) + 1) * (r // nb), n // nb - 1), 0)),
            pl.BlockSpec((SHORT_CONV_WIDTH, w), const),
            pl.BlockSpec((1, w), const),
            pl.BlockSpec(wg.shape, lambda i: (0, 0, 0)),
            pl.BlockSpec((1, w), const),
            pl.BlockSpec((1, w), const),
            pl.BlockSpec((1, w), const),
            pl.BlockSpec((nb, w), const),
        ],
        out_specs=[pl.BlockSpec((r, w), lambda i: (chunk(i), 0)),
                   pl.BlockSpec((nb, w), const)],
        out_shape=[jax.ShapeDtypeStruct((n, w), F32),
                   jax.ShapeDtypeStruct((nb, w), F32)],
        scratch_shapes=[pltpu.VMEM((lead + r + nb, w), F32),
                        pltpu.VMEM((r, w), F32),
                        pltpu.VMEM((r, w), F32),
                        pltpu.VMEM((nb, w), F32)],
        compiler_params=_params(1),
        name=name,
    )(xb, xb, xb, sconv_w, sconv_b.reshape(1, w), wg, b_r.reshape(1, w), b_i.reshape(1, w),
      lam.reshape(1, w), h0)


def _time_major(x_ref):
    nb, steps, d = x_ref.shape
    return jnp.swapaxes(x_ref[...], 0, 1).reshape(steps * nb, d)


def _ctx_inproj_kernel(x_ref, g_ref, shift_ref, scale_ref, w_ref, o_ref, *, nb):
    hx = _modulated_rmsnorm(_time_major(x_ref), g_ref[...], shift_ref[...], scale_ref[...],
                            nb).astype(BF16)
    o_ref[...] = jnp.dot(hx, w_ref[...], preferred_element_type=F32).astype(o_ref.dtype)


def _ctx_inproj(x, norm_g, shift, scale, w_bf16, steps):
    nb, seq, d = x.shape
    n = seq * nb
    tm = steps * nb
    width = w_bf16.shape[1]
    const = lambda i: (0, 0)
    return pl.pallas_call(
        functools.partial(_ctx_inproj_kernel, nb=nb),
        grid=(n // tm,),
        in_specs=[pl.BlockSpec((nb, steps, d), lambda i: (0, i, 0)),
                  pl.BlockSpec((1, d), const),
                  pl.BlockSpec((nb, d), const),
                  pl.BlockSpec((nb, d), const),
                  _resident(w_bf16.shape, const)],
        out_specs=pl.BlockSpec((tm, width), lambda i: (i, 0)),
        out_shape=jax.ShapeDtypeStruct((n, width), BF16),
        compiler_params=_params(1),
        name="ctx_inproj",
    )(x, norm_g.reshape(1, d), shift, scale, w_bf16)


def _even_in_kernel(x_ref, g_ref, shift_ref, scale_ref, w_ref, cw_ref, cb_ref, wg_ref, br_ref,
                    bi_ref, lam_ref, h0_ref, rows_ref, u_ref, sga_ref, sgb_ref, sc_ref, hs_ref,
                    ht_ref, xe_s, a_s, b_s, h_s, *, nb):
    c = pl.program_id(0)
    nc = pl.num_programs(0) - 1
    r, w = u_ref.shape
    lead = SHORT_CONV_LEAD * nb

    @pl.when(c == 0)
    def _():
        h_s[...] = h0_ref[...]
        xe_s[...] = jnp.zeros(xe_s.shape, F32)

    xe_s[0:lead + r] = xe_s[r:lead + 2 * r]

    x = _time_major(x_ref)
    rows_ref[...] = x
    hx = _modulated_rmsnorm(x, g_ref[...], shift_ref[...], scale_ref[...], nb).astype(BF16)

    def proj(group, cols=slice(0, w)):
        return jnp.dot(hx, w_ref[:, group * w + cols.start:group * w + cols.stop],
                       preferred_element_type=F32)

    xe_s[lead + r:lead + 2 * r] = proj(3) * (c < nc).astype(F32)

    q = w // (N_LRU_HEADS // 2)

    def proj_slice(i):
        if i < N_LRU_HEADS // 2:
            cols = slice(i * q, (i + 1) * q)
            u_ref[:, cols] = (proj(0, cols) * (jnp.tanh(proj(1, cols)) + 1.0)).astype(u_ref.dtype)
        else:
            j = i - N_LRU_HEADS // 2
            o_ref, group = (sga_ref, 2) if j < N_LRU_HEADS // 4 else (sgb_ref, 4)
            cols = slice((j % (N_LRU_HEADS // 4)) * 2 * q, (j % (N_LRU_HEADS // 4) + 1) * 2 * q)
            o_ref[:, cols] = _silu_of_twice(proj(group, cols)).astype(o_ref.dtype)

    def conv_out(sl):
        uc = _short_conv(xe_s, r, nb, cw_ref, cb_ref, sl)
        sc_ref[:, sl] = uc.astype(sc_ref.dtype)
        return uc

    _lru_coeffs(conv_out, wg_ref, br_ref, bi_ref, lam_ref, a_s, b_s, between_heads=proj_slice)

    h_prev = h_s[...]
    hfin = _lru_steps(a_s, b_s, h_prev, hs_ref, r // nb, nb, reverse=False)
    hfin = jnp.where(c > 0, hfin, h_prev)
    h_s[...] = hfin
    ht_ref[...] = hfin


def _even_in(x, norm_g, shift, scale, w_in_bf16, sconv_w, sconv_b, wg, b_r, b_i, lam, h0,
             *, steps):
    nb, seq, d = x.shape
    n = seq * nb
    w = sconv_w.shape[-1]
    r = steps * nb
    nc = n // r
    lead = SHORT_CONV_LEAD * nb
    assert w_in_bf16.shape == (d, 5 * w)
    const = lambda i: (0, 0)
    cur = lambda i: (jnp.minimum(i, nc - 1), 0)
    delayed = lambda i: (jnp.maximum(i - 1, 0), 0)
    kern = functools.partial(_even_in_kernel, nb=nb)
    act = jax.ShapeDtypeStruct((n, w), BF16)
    return pl.pallas_call(
        kern,
        grid=(nc + 1,),
        in_specs=[pl.BlockSpec((nb, steps, d), lambda i: (0, jnp.minimum(i, nc - 1), 0)),
                  pl.BlockSpec((1, d), const),
                  pl.BlockSpec((nb, d), const),
                  pl.BlockSpec((nb, d), const),
                  _resident(w_in_bf16.shape, const),
                  pl.BlockSpec((SHORT_CONV_WIDTH, w), const),
                  pl.BlockSpec((1, w), const),
                  _resident(wg.shape, lambda i: (0, 0, 0)),
                  pl.BlockSpec((1, w), const),
                  pl.BlockSpec((1, w), const),
                  pl.BlockSpec((1, w), const),
                  pl.BlockSpec((nb, w), const)],
        out_specs=[pl.BlockSpec((r, d), cur),
                   pl.BlockSpec((r, w), cur),
                   pl.BlockSpec((r, w), cur),
                   pl.BlockSpec((r, w), cur),
                   pl.BlockSpec((r, w), delayed),
                   pl.BlockSpec((r, w), delayed),
                   pl.BlockSpec((nb, w), const)],
        out_shape=[jax.ShapeDtypeStruct((n, d), F32), act, act, act, act,
                   jax.ShapeDtypeStruct((n, w), F32),
                   jax.ShapeDtypeStruct((nb, w), F32)],
        scratch_shapes=[pltpu.VMEM((lead + 2 * r, w), F32),
                        pltpu.VMEM((r, w), F32),
                        pltpu.VMEM((r, w), F32),
                        pltpu.VMEM((nb, w), F32)],
        compiler_params=_params(1),
        name="even_in",
    )(x, norm_g.reshape(1, d), shift, scale, w_in_bf16, sconv_w, sconv_b.reshape(1, w), wg,
      b_r.reshape(1, w), b_i.reshape(1, w), lam.reshape(1, w), h0)


def _even_out_kernel(uprev_ref, ucur_ref, unext_ref, sga_ref, sgb_ref, hf_ref, x_ref, gate_ref,
                     cw_ref, cb_ref, lg_ref, lb_ref, wo_ref, sc_ref, wg_ref, br_ref, bi_ref,
                     lam_ref, h0_ref, o_ref, ue_s, v_s, wrep_s, a_s, b_s, h_s, out_s,
                     *, nb, row_tile):
    i = pl.program_id(0)
    nc = pl.num_programs(0)
    c = nc - 1 - i
    r, w = ucur_ref.shape
    d = o_ref.shape[-1]
    halo = CONV_HALF * nb
    n_lane_tiles = w // LANES

    @pl.when(i == 0)
    def _():
        h_s[...] = h0_ref[...]
        for k in range(CONV_WIDTH):
            wrep_s[k] = jnp.broadcast_to(cw_ref[k:k + 1, :], (SUBLANES, w))

    def copy_in(dst_rows, src_ref):
        for lt in range(n_lane_tiles):
            ue_s[lt, dst_rows] = src_ref[:, lt * LANES:(lt + 1) * LANES].astype(F32)

    def fill(dst_rows, src_ref, valid):
        pl.when(valid)(lambda: copy_in(dst_rows, src_ref))

        @pl.when(jnp.logical_not(valid))
        def _():
            for lt in range(n_lane_tiles):
                ue_s[lt, dst_rows] = jnp.zeros((r, LANES), F32)

    fill(slice(0, r), uprev_ref, c > 0)
    copy_in(slice(r, 2 * r), ucur_ref)
    fill(slice(2 * r, 3 * r), unext_ref, c < nc - 1)

    n_row_tiles = r // row_tile
    sub = row_tile // SUBLANES
    for lt in range(n_lane_tiles):
        ls = slice(lt * LANES, (lt + 1) * LANES)
        bias = jnp.broadcast_to(cb_ref[:, ls], (SUBLANES, LANES))[None]

        def body(j, carry, lt=lt, ls=ls, bias=bias):
            base = pl.multiple_of(j * row_tile, row_tile) + (r - halo)

            def term(k):
                tap = ue_s[lt, pl.ds(base + k * nb, row_tile), :].reshape(sub, SUBLANES, LANES)
                return wrep_s[k, :, ls][None] * tap

            acc = jnp.broadcast_to(bias, (sub, SUBLANES, LANES))
            parked = None
            for k in range(CONV_WIDTH):
                acc = acc + term(k)
                if k % TAPS_PER_FENCE == TAPS_PER_FENCE - 1 and k < CONV_WIDTH - 1:
                    if parked is not None:
                        ue_s[lt, 3 * r:3 * r + row_tile, :] = parked.reshape(row_tile, LANES)
                    parked = acc
            v_s[lt, pl.ds(pl.multiple_of(j * row_tile, row_tile), row_tile), :] = (
                acc.reshape(row_tile, LANES))
            return carry

        lax.fori_loop(0, n_row_tiles, body, 0)

    v = jnp.concatenate([v_s[lt] for lt in range(n_lane_tiles)], axis=-1)
    mu = jnp.mean(v, axis=-1, keepdims=True)
    vc = v - mu
    var = jnp.mean(vc * vc, axis=-1, keepdims=True)
    yh = vc * lax.rsqrt(var + LN_EPS) * (0.5 * lg_ref[...]) + 0.5 * lb_ref[...]
    ya = (_silu_of_twice(yh) * sga_ref[...].astype(F32)).astype(BF16)

    q = d // (N_LRU_HEADS // 2)

    def out_slice(h):
        if h % 2 == 1:
            cols = slice((h // 2) * q, (h // 2 + 1) * q)
            out_s[:, cols] = jnp.dot(ya, wo_ref[0:w, cols], preferred_element_type=F32)

    _lru_coeffs(lambda sl: sc_ref[:, sl].astype(F32), wg_ref, br_ref, bi_ref, lam_ref, a_s, b_s,
                between_heads=out_slice)
    h_s[...] = _lru_steps(a_s, b_s, h_s[...], b_s, r // nb, nb, reverse=True)

    yb = ((hf_ref[...] + b_s[...]) * sgb_ref[...].astype(F32)).astype(BF16)
    out = out_s[...] + jnp.dot(yb, wo_ref[w:2 * w, :], preferred_element_type=F32)
    x = x_ref[...].reshape(r // nb, nb, d)
    o_ref[...] = (x + gate_ref[...][None] * out.reshape(r // nb, nb, d)).reshape(r, d)


def _even_out(u, sga, sgb, hf, sc, rows, gate, conv_w, conv_b, ln_g, ln_b, w_out_bf16,
              wg, b_r, b_i, lam, h0, *, steps):
    n, w = u.shape
    d = rows.shape[1]
    nb = gate.shape[0]
    r = steps * nb
    nc = n // r
    assert steps >= CONV_HALF
    const = lambda i: (0, 0)
    chunk = lambda i: nc - 1 - i
    tile = lambda i: (chunk(i), 0)
    row_tile = CONV_ROW_TILE
    kern = functools.partial(_even_out_kernel, nb=nb, row_tile=row_tile)
    return pl.pallas_call(
        kern,
        grid=(nc,),
        in_specs=[
            pl.BlockSpec((r, w), lambda i: (jnp.maximum(chunk(i) - 1, 0), 0)),
            pl.BlockSpec((r, w), tile),
            pl.BlockSpec((r, w), lambda i: (jnp.minimum(chunk(i) + 1, nc - 1), 0)),
            pl.BlockSpec((r, w), tile),
            pl.BlockSpec((r, w), tile),
            pl.BlockSpec((r, w), tile),
            pl.BlockSpec((r, d), tile),
            pl.BlockSpec((nb, d), const),
            pl.BlockSpec((CONV_WIDTH, w), const),
            pl.BlockSpec((1, w), const),
            pl.BlockSpec((1, w), const),
            pl.BlockSpec((1, w), const),
            _resident(w_out_bf16.shape, const),
            pl.BlockSpec((r, w), tile),
            _resident(wg.shape, lambda i: (0, 0, 0)),
            pl.BlockSpec((1, w), const),
            pl.BlockSpec((1, w), const),
            pl.BlockSpec((1, w), const),
            pl.BlockSpec((nb, w), const),
        ],
        out_specs=pl.BlockSpec((r, d), tile),
        out_shape=jax.ShapeDtypeStruct((n, d), F32),
        scratch_shapes=[pltpu.VMEM((w // LANES, 3 * r + row_tile, LANES), F32),
                        pltpu.VMEM((w // LANES, r, LANES), F32),
                        pltpu.VMEM((CONV_WIDTH, SUBLANES, w), F32),
                        pltpu.VMEM((r, w), F32),
                        pltpu.VMEM((r, w), F32),
                        pltpu.VMEM((nb, w), F32),
                        pltpu.VMEM((r, d), F32)],
        compiler_params=_params(1),
        name="even_out",
    )(u, u, u, sga, sgb, hf, rows, gate, conv_w, conv_b.reshape(1, w), ln_g.reshape(1, w),
      ln_b.reshape(1, w), w_out_bf16, sc, wg, b_r.reshape(1, w), b_i.reshape(1, w),
      lam.reshape(1, w), h0)


def _odd_kernel(x_ref, ng_ref, shift_ref, scale_ref, gate_ref, win_ref, wgrp_ref, pscale_ref,
                wo_ref, fg_ref, o_ref, ue_s, inv_s, *, nbb):
    first = (pl.program_id(0) == 0) & (pl.program_id(1) == 0)
    gw, _, d = x_ref.shape
    n_groups = len(POOL_WINDOWS)
    wp = pscale_ref.shape[-1]
    gd = wp // n_groups
    r = gw * nbb
    pad = (max(POOL_WINDOWS) // 2) * nbb

    @pl.when(first)
    def _():
        t = lax.broadcasted_iota(jnp.int32, (r, LANES), 0) // nbb
        for gi, win in enumerate(POOL_WINDOWS):
            half = win // 2
            cnt = jnp.minimum(t + half, gw) - jnp.maximum(t - half, 0)
            inv_s[gi] = 1.0 / cnt.astype(F32)
        zeros = jnp.zeros((pad, gd), F32)
        for buf in range(ue_s.shape[0]):
            ue_s[buf, 0:pad] = zeros
            ue_s[buf, pad + r:pad + r + pad] = zeros

    x = x_ref[...]
    hx = _modulated_rmsnorm(x.reshape(r, d), ng_ref[...], shift_ref[...], scale_ref[...],
                            nbb).astype(BF16)
    acc = None
    for gi, win in enumerate(POOL_WINDOWS):
        gs = slice(gi * gd, (gi + 1) * gd)
        buf = gi % ue_s.shape[0]
        ue_s[buf, pad:pad + r] = jnp.dot(hx, win_ref[:, gs], preferred_element_type=F32)
        sgate = _silu_of_twice(jnp.dot(hx, win_ref[:, wp + gi * gd:wp + (gi + 1) * gd],
                                       preferred_element_type=F32))
        e = ue_s[buf]
        cur = e[0:r + 2 * pad - nbb] + e[nbb:r + 2 * pad]
        lo = 1
        span = 2
        while span < win:
            sh = (span // 2) * nbb
            cur = cur[0:cur.shape[0] - 2 * sh] + cur[2 * sh:]
            lo += span // 2
            span *= 2
        start = pad - lo * nbb
        inv = jnp.concatenate([inv_s[gi]] * (gd // LANES), axis=-1)
        dlt = (cur[start:start + r] * inv - ue_s[buf, pad:pad + r]).astype(BF16)
        y = jnp.dot(dlt, wgrp_ref[gi], preferred_element_type=F32)
        y = y * pscale_ref[:, gs] * sgate
        part = jnp.dot(y.astype(BF16), wo_ref[gs, :], preferred_element_type=F32)
        acc = part if acc is None else acc + part

    xo = x + gate_ref[...][None] * acc.reshape(gw, nbb, d)
    ms = jnp.mean(xo * xo, axis=-1, keepdims=True)
    o_ref[...] = jnp.swapaxes(xo * lax.rsqrt(ms + RMS_EPS) * fg_ref[...][None], 0, 1)


def _odd_layer(rows3, norm_g, shift, scale, gate, w_in_bf16, w_grp_bf16, pscale, w_out_bf16,
               final_g, *, nbb):
    seq, nb, d = rows3.shape
    n_groups = len(POOL_WINDOWS)
    wp = pscale.shape[-1]
    gd = wp // n_groups
    pad = (max(POOL_WINDOWS) // 2) * nbb
    r = GRID_W * nbb
    blk = lambda i, j: (i, j, 0)
    per_batch = lambda i, j: (j, 0)
    const = lambda i, j: (0, 0)
    kern = functools.partial(_odd_kernel, nbb=nbb)
    return pl.pallas_call(
        kern,
        grid=(seq // GRID_W, nb // nbb),
        in_specs=[
            pl.BlockSpec((GRID_W, nbb, d), blk),
            pl.BlockSpec((1, d), const),
            pl.BlockSpec((nbb, d), per_batch),
            pl.BlockSpec((nbb, d), per_batch),
            pl.BlockSpec((nbb, d), per_batch),
            _resident(w_in_bf16.shape, const),
            _resident(w_grp_bf16.shape, lambda i, j: (0, 0, 0)),
            pl.BlockSpec((1, wp), const),
            _resident(w_out_bf16.shape, const),
            pl.BlockSpec((1, d), const),
        ],
        out_specs=pl.BlockSpec((nbb, GRID_W, d), lambda i, j: (j, i, 0)),
        out_shape=jax.ShapeDtypeStruct((nb, seq, d), F32),
        scratch_shapes=[pltpu.VMEM((2, r + 2 * pad, gd), F32),
                        pltpu.VMEM((n_groups, r, LANES), F32)],
        compiler_params=_params(2),
        name="odd_layer",
    )(rows3, norm_g.reshape(1, d), shift, scale, gate, w_in_bf16, w_grp_bf16,
      pscale.reshape(1, wp), w_out_bf16, final_g.reshape(1, d))


def kernel(x, c, ctx, c_ctx, norm_g, mod_w, mod_b, ev_w_in, ev_conv_w, ev_conv_b, ev_ln_g, ev_ln_b,
           ev_sconv_w, ev_sconv_b, ev_w_r, ev_b_r, ev_w_i, ev_b_i, ev_lam, ev_w_out,
           od_w_in, od_w_grp, od_scale, od_w_out, final_g):
    bn, seq, d = x.shape
    w_conv = ev_conv_w.shape[-1]
    w_lru = ev_sconv_w.shape[-1]
    assert w_conv == w_lru

    n_cond = -(-(bn + 1) // SUBLANES) * SUBLANES
    cc = jnp.zeros((n_cond, d), F32).at[:bn].set(c).at[bn].set(c_ctx)
    mod = _modulation(cc, mod_w, mod_b)
    shift0, scale0, gate0 = (mod[0, :bn, k * d:(k + 1) * d] for k in range(3))
    shift1, scale1, gate1 = (mod[1, :bn, k * d:(k + 1) * d] for k in range(3))
    shift_c = jnp.broadcast_to(mod[0, bn:bn + 1, 0:d], (bn, d))
    scale_c = jnp.broadcast_to(mod[0, bn:bn + 1, d:2 * d], (bn, d))

    xb_col = 3 * w_conv
    halved = jnp.full((ev_w_in.shape[-1],), 0.5, F32).at[xb_col:xb_col + w_lru].set(1.0)
    w_in = (ev_w_in[0] * halved).astype(BF16)
    xb_ctx = _ctx_inproj(ctx, norm_g[0], shift_c, scale_c, w_in[:, xb_col:xb_col + w_lru],
                         CHUNK_STEPS)

    def gate_params(k):
        wg = (0.5 * jnp.concatenate([ev_w_r[0, k], ev_w_i[0, k]], axis=-1)).astype(BF16)
        return dict(wg=wg, b_r=ev_b_r[0, k], b_i=ev_b_i[0, k], lam=ev_lam[0, k])

    sconv = dict(sconv_w=ev_sconv_w[0], sconv_b=ev_sconv_b[0])
    zero_state = jnp.zeros((bn, w_lru), F32)
    _, h_ctx_f = _lru_scan(xb_ctx, h0=zero_state, steps=CHUNK_STEPS, reverse=False,
                           name="ctx_scan0", **sconv, **gate_params(0))
    _, h_ctx_b = _lru_scan(xb_ctx, h0=zero_state, steps=CHUNK_STEPS, reverse=True,
                           name="ctx_scan1", **sconv, **gate_params(1))

    rows, u, sga, sgb, sc, hf, _ = _even_in(x, norm_g[0], shift0, scale0, w_in, h0=h_ctx_f,
                                            steps=CHUNK_STEPS, **sconv, **gate_params(0))
    rows = _even_out(u, sga, sgb, hf, sc, rows, gate0, ev_conv_w[0], ev_conv_b[0],
                     ev_ln_g[0], ev_ln_b[0], ev_w_out[0].astype(BF16), h0=h_ctx_b,
                     steps=CHUNK_STEPS, **gate_params(1))

    w_pool = od_scale.shape[-1]
    od_halved = jnp.ones((od_w_in.shape[-1],), F32).at[w_pool:].set(0.5)
    return _odd_layer(rows.reshape(seq, bn, d), norm_g[1], shift1, scale1, gate1,
                      (od_w_in[0] * od_halved).astype(BF16), od_w_grp[0].astype(BF16), od_scale[0],
                      od_w_out[0].astype(BF16), final_g, nbb=ODD_BATCH_ROWS)
```
